```python
import jax, jax.numpy as jnp
from jax import lax
import numpy as np

D_MODEL = 2048
BATCH = 16
SEQ = 2048
DEPTH = 2

N_MIXERS = 2
N_A_LAYERS = (DEPTH + 1) // 2
N_B_LAYERS = DEPTH // 2
PLE_DIM = 256
CHUNK = 128
A_WIDTH = D_MODEL
A_HEADS = 16
A_HEAD_DIM = A_WIDTH // A_HEADS
C_WIDTH = D_MODEL
CONV_K = 31
N_EXPERTS = 32
TOP_K = 4
D_EXPERT = D_MODEL
SWIGLU_LIMIT = 7.0
SWIGLU_ALPHA = 1.702
ROW_BLOCK = 128
RMS_EPS = 1e-5
LN_EPS = 1e-5

kernel_name = 'hybrid_gmlp_conformer_moe_ple'


def rms_norm(x, g):
    xf = x.astype(jnp.float32)
    y = xf * lax.rsqrt(jnp.mean(xf * xf, axis=-1, keepdims=True) + RMS_EPS)
    return (y * g.astype(jnp.float32)).astype(x.dtype)


def layer_norm(x, g, b):
    xf = x.astype(jnp.float32)
    xc = xf - jnp.mean(xf, axis=-1, keepdims=True)
    y = xc * lax.rsqrt(jnp.mean(xc * xc, axis=-1, keepdims=True) + LN_EPS)
    return (y * g.astype(jnp.float32) + b.astype(jnp.float32)).astype(x.dtype)


def chunked_sgu_mixer(hn, w_in, b_in, ln_g, ln_b, w_s, b_s, w_out, b_out):
    bsz, seq, _ = hn.shape
    z = jax.nn.gelu(hn @ w_in + b_in)
    u, v = jnp.split(z, 2, axis=-1)
    v = layer_norm(v, ln_g, ln_b)
    v = v.reshape(bsz, seq // CHUNK, CHUNK, A_HEADS, A_HEAD_DIM)
    causal = jnp.tril(jnp.ones((CHUNK, CHUNK), dtype=bool))
    w = jnp.where(causal[None], w_s, jnp.zeros_like(w_s))
    mixed = jnp.einsum('hts,bcshd->bcthd', w, v) + jnp.transpose(b_s)[None, None, :, :, None]
    gated = u * mixed.reshape(bsz, seq, A_WIDTH)
    return gated @ w_out + b_out


def conformer_conv_mixer(hn, w_in, b_in, dw_w, dw_b, ln_g, ln_b, w_out, b_out):
    a, g = jnp.split(hn @ w_in + b_in, 2, axis=-1)
    z = a * jax.nn.sigmoid(g)
    z = lax.conv_general_dilated(
        z, dw_w[:, None, :].astype(z.dtype), window_strides=(1,),
        padding=[(CONV_K - 1, 0)], dimension_numbers=('NWC', 'WIO', 'NWC'),
        feature_group_count=C_WIDTH) + dw_b
    z = jax.nn.silu(layer_norm(z, ln_g, ln_b))
    return z @ w_out + b_out


def moe_channel_mixer(hn, router_w, router_b, w_gu, b_gu, w_down, b_down):
    bsz, seq, d = hn.shape
    n_tok = bsz * seq
    h = hn.reshape(n_tok, d)
    logits = h.astype(jnp.float32) @ router_w.astype(jnp.float32) + router_b.astype(jnp.float32)
    top_logits, top_idx = lax.top_k(logits, TOP_K)
    gates = jax.nn.softmax(top_logits, axis=-1).astype(h.dtype)
    n_assign = n_tok * TOP_K
    flat_e = top_idx.reshape(n_assign)
    flat_tok = jnp.arange(n_assign, dtype=jnp.int32) // TOP_K
    flat_gate = gates.reshape(n_assign)
    order = jnp.argsort(flat_e, stable=True)
    sorted_e = flat_e[order]
    counts = jnp.bincount(flat_e, length=N_EXPERTS)
    padded = (counts + ROW_BLOCK - 1) // ROW_BLOCK * ROW_BLOCK
    starts = jnp.cumsum(counts) - counts
    pad_ends = jnp.cumsum(padded)
    pad_starts = pad_ends - padded
    dest = pad_starts[sorted_e] + jnp.arange(n_assign, dtype=jnp.int32) - starts[sorted_e]
    n_rows = -(-(n_assign + N_EXPERTS * (ROW_BLOCK - 1)) // ROW_BLOCK) * ROW_BLOCK
    n_blocks = n_rows // ROW_BLOCK
    row_tok = jnp.full((n_rows,), n_tok, jnp.int32).at[dest].set(flat_tok[order])
    row_gate = jnp.zeros((n_rows,), h.dtype).at[dest].set(flat_gate[order])
    block_e = jnp.minimum(
        jnp.searchsorted(pad_ends, jnp.arange(n_blocks, dtype=jnp.int32) * ROW_BLOCK, side='right'),
        N_EXPERTS - 1)
    h_pad = jnp.concatenate([h, jnp.zeros((1, d), h.dtype)], axis=0)

    def expert_block(args):
        tok, gate, e = args
        xb = h_pad[tok]
        gu = xb @ w_gu[e] + b_gu[e]
        g, u = jnp.split(gu, 2, axis=-1)
        g = jnp.minimum(g, SWIGLU_LIMIT)
        u = jnp.clip(u, -SWIGLU_LIMIT, SWIGLU_LIMIT)
        act = g * jax.nn.sigmoid(SWIGLU_ALPHA * g) * (u + 1)
        return (act @ w_down[e] + b_down[e]) * gate[:, None]

    ys = lax.map(expert_block, (row_tok.reshape(n_blocks, ROW_BLOCK),
                                row_gate.reshape(n_blocks, ROW_BLOCK), block_e))
    out = jax.ops.segment_sum(ys.reshape(n_rows, d), row_tok, num_segments=n_tok + 1)[:n_tok]
    return out.reshape(bsz, seq, d)


def per_layer_embedding(h, p_i, norm_g, w_proj, proj_norm_g, w_gate, b_gate):
    pe = rms_norm(p_i @ w_proj, proj_norm_g)
    gate = jax.nn.sigmoid(rms_norm(h, norm_g) @ w_gate + b_gate)
    return h + gate * pe


def setup_inputs(seed: int = 0) -> dict:
    key = jax.random.key(seed)
    ks = iter(jax.random.split(key, 40))

    def nrm(shape, scale):
        return jax.random.normal(next(ks), shape, jnp.float32) * scale

    def gain(shape):
        return 1.0 + nrm(shape, 0.02)

    D, E, F = D_MODEL, N_EXPERTS, D_EXPERT
    return {
        'x': nrm((BATCH, SEQ, D), 1.0),
        'p': nrm((DEPTH, BATCH, SEQ, PLE_DIM), 1.0),
        'mix_norm': gain((DEPTH, D)),
        'ffn_norm': gain((DEPTH, D)),
        'ple_norm': gain((DEPTH, D)),
        'final_norm': gain((D,)),
        'a_w_in': nrm((N_A_LAYERS, D, 2 * A_WIDTH), D ** -0.5),
        'a_b_in': nrm((N_A_LAYERS, 2 * A_WIDTH), 0.02),
        'a_ln_g': gain((N_A_LAYERS, A_WIDTH)),
        'a_ln_b': nrm((N_A_LAYERS, A_WIDTH), 0.02),
        'a_w_s': nrm((N_A_LAYERS, A_HEADS, CHUNK, CHUNK), CHUNK ** -0.5),
        'a_b_s': 1.0 + nrm((N_A_LAYERS, A_HEADS, CHUNK), 0.02),
        'a_w_out': nrm((N_A_LAYERS, A_WIDTH, D), A_WIDTH ** -0.5),
        'a_b_out': nrm((N_A_LAYERS, D), 0.02),
        'c_w_in': nrm((N_B_LAYERS, D, 2 * C_WIDTH), D ** -0.5),
        'c_b_in': nrm((N_B_LAYERS, 2 * C_WIDTH), 0.02),
        'c_dw_w': nrm((N_B_LAYERS, CONV_K, C_WIDTH), CONV_K ** -0.5),
        'c_dw_b': nrm((N_B_LAYERS, C_WIDTH), 0.02),
        'c_ln_g': gain((N_B_LAYERS, C_WIDTH)),
        'c_ln_b': nrm((N_B_LAYERS, C_WIDTH), 0.02),
        'c_w_out': nrm((N_B_LAYERS, C_WIDTH, D), C_WIDTH ** -0.5),
        'c_b_out': nrm((N_B_LAYERS, D), 0.02),
        'router_w': nrm((DEPTH, D, E), D ** -0.5),
        'router_b': nrm((DEPTH, E), 0.01),
        'e_w_gu': nrm((DEPTH, E, D, 2 * F), D ** -0.5),
        'e_b_gu': nrm((DEPTH, E, 2 * F), 0.02),
        'e_w_down': nrm((DEPTH, E, F, D), F ** -0.5),
        'e_b_down': nrm((DEPTH, E, D), 0.02),
        'ple_w_proj': nrm((DEPTH, PLE_DIM, D), PLE_DIM ** -0.5),
        'ple_proj_norm': gain((DEPTH, D)),
        'ple_w_gate': nrm((DEPTH, D, D), D ** -0.5),
        'ple_b_gate': nrm((DEPTH, D), 0.02),
    }


def reference(x, p, mix_norm, ffn_norm, ple_norm, final_norm,
              a_w_in, a_b_in, a_ln_g, a_ln_b, a_w_s, a_b_s, a_w_out, a_b_out,
              c_w_in, c_b_in, c_dw_w, c_dw_b, c_ln_g, c_ln_b, c_w_out, c_b_out,
              router_w, router_b, e_w_gu, e_b_gu, e_w_down, e_b_down,
              ple_w_proj, ple_proj_norm, ple_w_gate, ple_b_gate):
    h = x
    for i in range(DEPTH):
        hn = rms_norm(h, mix_norm[i])
        j = i // N_MIXERS
        if i % N_MIXERS == 0:
            h = h + chunked_sgu_mixer(hn, a_w_in[j], a_b_in[j], a_ln_g[j], a_ln_b[j],
                                      a_w_s[j], a_b_s[j], a_w_out[j], a_b_out[j])
        else:
            h = h + conformer_conv_mixer(hn, c_w_in[j], c_b_in[j], c_dw_w[j], c_dw_b[j],
                                         c_ln_g[j], c_ln_b[j], c_w_out[j], c_b_out[j])
        h = h + moe_channel_mixer(rms_norm(h, ffn_norm[i]), router_w[i], router_b[i],
                                  e_w_gu[i], e_b_gu[i], e_w_down[i], e_b_down[i])
        h = per_layer_embedding(h, p[i], ple_norm[i], ple_w_proj[i], ple_proj_norm[i],
                                ple_w_gate[i], ple_b_gate[i])
    return rms_norm(h, final_norm)
```

```python
import functools

import jax
import jax.numpy as jnp
from jax import lax
from jax.experimental import pallas as pl
from jax.experimental.pallas import tpu as pltpu

CHUNK = 128
A_HEADS = 16
CONV_K = 31
TOP_K = 4
SWIGLU_LIMIT = 7.0
SWIGLU_ALPHA = 1.702
RMS_EPS = 1e-5
LN_EPS = 1e-5

LANES = 128
SUBLANES = 8
VMEM_LIMIT_BYTES = 56 * 1024 * 1024

IN_TM = 1024
IN_TN = 1024
MIX_TM = 256
CONV_HALO = 32
ROUTER_TM = 512
ROW_TM = 256
EXP_TM = 512
GU_TF = 512
DN_TN = 1024

F32 = jnp.float32
BF16 = jnp.bfloat16


def _cparams(sems):
    return pltpu.CompilerParams(dimension_semantics=sems, vmem_limit_bytes=VMEM_LIMIT_BYTES)


def _rms(x, g):
    return x * lax.rsqrt(jnp.mean(x * x, axis=-1, keepdims=True) + RMS_EPS) * g


def _layer_norm(x, g, b):
    xc = x - jnp.mean(x, axis=-1, keepdims=True)
    return xc * lax.rsqrt(jnp.mean(xc * xc, axis=-1, keepdims=True) + LN_EPS) * g + b


def _in_proj_gelu_kernel(h_ref, g_ref, w_ref, b_ref, o_ref, hn_ref):
    @pl.when(pl.program_id(1) == 0)
    def _():
        hn_ref[...] = _rms(h_ref[...], g_ref[...]).astype(BF16)

    z = jnp.dot(hn_ref[...], w_ref[...], preferred_element_type=F32) + b_ref[...]
    o_ref[...] = jax.nn.gelu(z).astype(o_ref.dtype)


def _in_proj_glu_kernel(h_ref, g_ref, wa_ref, wg_ref, ba_ref, bg_ref, o_ref, hn_ref):
    @pl.when(pl.program_id(1) == 0)
    def _():
        hn_ref[...] = _rms(h_ref[...], g_ref[...]).astype(BF16)

    hn = hn_ref[...]
    a = jnp.dot(hn, wa_ref[...], preferred_element_type=F32) + ba_ref[...]
    g = jnp.dot(hn, wg_ref[...], preferred_element_type=F32) + bg_ref[...]
    o_ref[...] = (a * jax.nn.sigmoid(g)).astype(o_ref.dtype)


def _in_proj_gelu(h, norm_g, w, b):
    n, d = h.shape
    cols = w.shape[1]
    tm, tn = min(IN_TM, n), IN_TN
    return pl.pallas_call(
        _in_proj_gelu_kernel,
        grid=(n // tm, cols // tn),
        in_specs=[
            pl.BlockSpec((tm, d), lambda i, j: (i, 0)),
            pl.BlockSpec((1, d), lambda i, j: (0, 0)),
            pl.BlockSpec((d, tn), lambda i, j: (0, j)),
            pl.BlockSpec((1, tn), lambda i, j: (0, j)),
        ],
        out_specs=pl.BlockSpec((tm, tn), lambda i, j: (i, j)),
        out_shape=jax.ShapeDtypeStruct((n, cols), BF16),
        scratch_shapes=[pltpu.VMEM((tm, d), BF16)],
        compiler_params=_cparams(("arbitrary", "arbitrary")),
        name="in_proj_gelu",
    )(h, norm_g.reshape(1, d), w.astype(BF16), b.reshape(1, cols))


def _in_proj_glu(h, norm_g, w, b):
    n, d = h.shape
    width = w.shape[1] // 2
    tm, tn = min(IN_TM, n), IN_TN
    nj = width // tn
    wb = w.astype(BF16)
    b2 = b.reshape(1, 2 * width)
    return pl.pallas_call(
        _in_proj_glu_kernel,
        grid=(n // tm, nj),
        in_specs=[
            pl.BlockSpec((tm, d), lambda i, j: (i, 0)),
            pl.BlockSpec((1, d), lambda i, j: (0, 0)),
            pl.BlockSpec((d, tn), lambda i, j: (0, j)),
            pl.BlockSpec((d, tn), lambda i, j: (0, j + nj)),
            pl.BlockSpec((1, tn), lambda i, j: (0, j)),
            pl.BlockSpec((1, tn), lambda i, j: (0, j + nj)),
        ],
        out_specs=pl.BlockSpec((tm, tn), lambda i, j: (i, j)),
        out_shape=jax.ShapeDtypeStruct((n, width), BF16),
        scratch_shapes=[pltpu.VMEM((tm, d), BF16)],
        compiler_params=_cparams(("arbitrary", "arbitrary")),
        name="in_proj_glu",
    )(h, norm_g.reshape(1, d), wb, wb, b2, b2)


def _sgu_out_kernel(u_ref, v_ref, lng_ref, lnb_ref, ws_ref, bsb_ref, wo_ref, bo_ref, h_ref,
                    o_ref, mixed_ref):
    tm = u_ref.shape[0]
    nchunk = tm // CHUNK
    vn = _layer_norm(v_ref[...].astype(F32), lng_ref[...], lnb_ref[...]).astype(BF16)
    tgt = lax.broadcasted_iota(jnp.int32, (CHUNK, CHUNK), 0)
    src = lax.broadcasted_iota(jnp.int32, (CHUNK, CHUNK), 1)
    causal = src <= tgt
    for hd in range(A_HEADS):
        hs = slice(hd * LANES, (hd + 1) * LANES)
        w = jnp.where(causal, ws_ref[hd], 0.0).astype(BF16)
        rhs = jnp.concatenate([vn[c * CHUNK:(c + 1) * CHUNK, hs] for c in range(nchunk)], axis=1)
        mix = jnp.dot(w, rhs, preferred_element_type=F32)
        for c in range(nchunk):
            mixed_ref[c * CHUNK:(c + 1) * CHUNK, hs] = mix[:, c * LANES:(c + 1) * LANES] + bsb_ref[:, hs]
    gated = (u_ref[...].astype(F32) * mixed_ref[...]).astype(BF16)
    o_ref[...] = h_ref[...] + (jnp.dot(gated, wo_ref[...], preferred_element_type=F32) + bo_ref[...])


def _sgu_out(z, ln_g, ln_b, w_s, b_s, w_out, b_out, h):
    n, d = h.shape
    width = z.shape[1] // 2
    assert width // A_HEADS == LANES
    tm = MIX_TM
    bsb = jnp.repeat(jnp.transpose(b_s), LANES, axis=1)
    return pl.pallas_call(
        _sgu_out_kernel,
        grid=(n // tm,),
        in_specs=[
            pl.BlockSpec((tm, width), lambda i: (i, 0)),
            pl.BlockSpec((tm, width), lambda i: (i, 1)),
            pl.BlockSpec((1, width), lambda i: (0, 0)),
            pl.BlockSpec((1, width), lambda i: (0, 0)),
            pl.BlockSpec((A_HEADS, CHUNK, CHUNK), lambda i: (0, 0, 0)),
            pl.BlockSpec((CHUNK, width), lambda i: (0, 0)),
            pl.BlockSpec((width, d), lambda i: (0, 0)),
            pl.BlockSpec((1, d), lambda i: (0, 0)),
            pl.BlockSpec((tm, d), lambda i: (i, 0)),
        ],
        out_specs=pl.BlockSpec((tm, d), lambda i: (i, 0)),
        out_shape=jax.ShapeDtypeStruct((n, d), F32),
        scratch_shapes=[pltpu.VMEM((tm, width), F32)],
        compiler_params=_cparams(("arbitrary",)),
        name="sgu_out",
    )(z, z, ln_g.reshape(1, width), ln_b.reshape(1, width), w_s, bsb, w_out.astype(BF16),
      b_out.reshape(1, d), h)


def _conv_out_kernel(z_ref, halo_ref, dww_ref, dwb_ref, lng_ref, lnb_ref, wo_ref, bo_ref, h_ref,
                     o_ref, zbuf_ref, acc_ref, *, tiles_per_seq):
    tm, width = z_ref.shape
    first = (pl.program_id(0) % tiles_per_seq) == 0
    zbuf_ref[0:CONV_HALO, :] = jnp.where(first, 0.0, halo_ref[...].astype(F32))
    zbuf_ref[CONV_HALO:CONV_HALO + tm, :] = z_ref[...].astype(F32)

    def lane_tile(l, carry):
        ls = pl.ds(pl.multiple_of(l * LANES, LANES), LANES)
        acc = jnp.zeros((tm, LANES), F32)
        for k in range(CONV_K):
            off = CONV_HALO - (CONV_K - 1) + k
            acc = acc + dww_ref[k:k + 1, ls] * zbuf_ref[off:off + tm, ls]
        acc_ref[:, ls] = acc + dwb_ref[:, ls]
        return carry

    lax.fori_loop(0, width // LANES, lane_tile, 0)
    y = jax.nn.silu(_layer_norm(acc_ref[...], lng_ref[...], lnb_ref[...])).astype(BF16)
    o_ref[...] = h_ref[...] + (jnp.dot(y, wo_ref[...], preferred_element_type=F32) + bo_ref[...])


def _conv_out(z, seq, dw_w, dw_b, ln_g, ln_b, w_out, b_out, h):
    n, d = h.shape
    width = z.shape[1]
    tm = MIX_TM
    assert seq % tm == 0 and tm % CONV_HALO == 0 and CONV_HALO >= CONV_K - 1
    ratio = tm // CONV_HALO
    return pl.pallas_call(
        functools.partial(_conv_out_kernel, tiles_per_seq=seq // tm),
        grid=(n // tm,),
        in_specs=[
            pl.BlockSpec((tm, width), lambda i: (i, 0)),
            pl.BlockSpec((CONV_HALO, width), lambda i: (jnp.maximum(i * ratio - 1, 0), 0)),
            pl.BlockSpec((CONV_K, width), lambda i: (0, 0)),
            pl.BlockSpec((1, width), lambda i: (0, 0)),
            pl.BlockSpec((1, width), lambda i: (0, 0)),
            pl.BlockSpec((1, width), lambda i: (0, 0)),
            pl.BlockSpec((width, d), lambda i: (0, 0)),
            pl.BlockSpec((1, d), lambda i: (0, 0)),
            pl.BlockSpec((tm, d), lambda i: (i, 0)),
        ],
        out_specs=pl.BlockSpec((tm, d), lambda i: (i, 0)),
        out_shape=jax.ShapeDtypeStruct((n, d), F32),
        scratch_shapes=[pltpu.VMEM((CONV_HALO + tm, width), F32), pltpu.VMEM((tm, width), F32)],
        compiler_params=_cparams(("arbitrary",)),
        name="conv_out",
    )(z, z, dw_w, dw_b.reshape(1, width), ln_g.reshape(1, width), ln_b.reshape(1, width),
      w_out.astype(BF16), b_out.reshape(1, d), h)


def _router_kernel(h_ref, g_ref, rwt_ref, rb_ref, idx_ref, gate_ref, rank_ref, cnt_ref, carry_ref):
    tm = h_ref.shape[0]
    n_exp = rwt_ref.shape[0]

    @pl.when(pl.program_id(0) == 0)
    def _():
        carry_ref[...] = jnp.zeros_like(carry_ref)

    hn = _rms(h_ref[...], g_ref[...])
    logits = lax.dot_general(rwt_ref[...], hn, (((1,), (1,)), ((), ())),
                             precision=lax.Precision.HIGHEST,
                             preferred_element_type=F32) + rb_ref[...]
    eio = lax.broadcasted_iota(jnp.int32, (n_exp, tm), 0)
    cur = logits
    sels, tops = [], []
    for j in range(TOP_K):
        m = jnp.max(cur, axis=0, keepdims=True)
        idx = jnp.min(jnp.where(cur == m, eio, n_exp), axis=0, keepdims=True)
        sel = eio == idx
        idx_ref[j:j + 1, :] = idx
        sels.append(sel)
        tops.append(m)
        cur = jnp.where(sel, -jnp.inf, cur)
    exps = [jnp.exp(t - tops[0]) for t in tops]
    denom = exps[0] + exps[1] + exps[2] + exps[3]
    for j in range(TOP_K):
        gate_ref[j:j + 1, :] = exps[j] / denom
    onehot = jnp.zeros((n_exp, tm), F32)
    for sel in sels:
        onehot = onehot + sel.astype(F32)
    before = (lax.broadcasted_iota(jnp.int32, (tm, tm), 0)
              < lax.broadcasted_iota(jnp.int32, (tm, tm), 1)).astype(BF16)
    prefix = jnp.dot(onehot.astype(BF16), before, preferred_element_type=F32)
    carry = carry_ref[...]
    rank_all = prefix + jnp.concatenate([carry] * (tm // LANES), axis=1)
    for j in range(TOP_K):
        rank_ref[j:j + 1, :] = jnp.sum(jnp.where(sels[j], rank_all, 0.0), axis=0,
                                       keepdims=True).astype(jnp.int32)
    carry = carry + jnp.sum(onehot, axis=1, keepdims=True)
    carry_ref[...] = carry
    cnt_ref[...] = carry.astype(jnp.int32)


def _router(h, norm_g, router_w, router_b):
    n, d = h.shape
    n_exp = router_w.shape[1]
    tm = min(ROUTER_TM, n)
    out_spec = pl.BlockSpec((TOP_K, tm), lambda i: (0, i))
    return pl.pallas_call(
        _router_kernel,
        grid=(n // tm,),
        in_specs=[
            pl.BlockSpec((tm, d), lambda i: (i, 0)),
            pl.BlockSpec((1, d), lambda i: (0, 0)),
            pl.BlockSpec((n_exp, d), lambda i: (0, 0)),
            pl.BlockSpec((n_exp, 1), lambda i: (0, 0)),
        ],
        out_specs=[out_spec, out_spec, out_spec, pl.BlockSpec((n_exp, LANES), lambda i: (0, 0))],
        out_shape=[
            jax.ShapeDtypeStruct((TOP_K, n), jnp.int32),
            jax.ShapeDtypeStruct((TOP_K, n), F32),
            jax.ShapeDtypeStruct((TOP_K, n), jnp.int32),
            jax.ShapeDtypeStruct((n_exp, LANES), jnp.int32),
        ],
        scratch_shapes=[pltpu.VMEM((n_exp, LANES), F32)],
        compiler_params=_cparams(("arbitrary",)),
        name="router",
    )(h, norm_g.reshape(1, d), jnp.transpose(router_w), router_b.reshape(n_exp, 1))


def _dispatch_kernel(vend_ref, pend_ref, h_ref, g_ref, dest_ref, xs_hbm, hn_ref, sems, zsem, *, nsteps):
    tm = h_ref.shape[0]
    n_exp = vend_ref.shape[0]
    n_rows = xs_hbm.shape[0]
    i = pl.program_id(0)
    slot = i % 2

    def wait_slot(s):
        for _ in range(TOP_K):
            pltpu.make_async_copy(hn_ref.at[s], xs_hbm.at[pl.ds(0, tm)], sems.at[s]).wait()

    @pl.when(i >= 2)
    def _():
        wait_slot(slot)

    hn_ref[slot] = _rms(h_ref[...], g_ref[...])

    def issue(t, carry):
        for j in range(TOP_K):
            d = dest_ref[t * TOP_K + j]
            pltpu.make_async_copy(hn_ref.at[slot, pl.ds(t, 1)], xs_hbm.at[pl.ds(d, 1)],
                                  sems.at[slot]).start()
        return carry

    lax.fori_loop(0, tm, issue, 0)

    @pl.when(i == nsteps - 1)
    def _():
        wait_slot(slot)
        if nsteps > 1:
            wait_slot(1 - slot)
        hn_ref[0] = jnp.zeros(hn_ref.shape[1:], F32)

        def zero_row(r):
            return pltpu.make_async_copy(hn_ref.at[0, pl.ds(0, 1)], xs_hbm.at[pl.ds(r, 1)], zsem)

        def zero_chunk(c):
            return pltpu.make_async_copy(hn_ref.at[0], xs_hbm.at[pl.ds(c * tm, tm)], zsem)

        def for_range(lo, hi, fn):
            def body(r, carry):
                fn(r)
                return carry

            lax.fori_loop(lo, hi, body, 0)

        def per_expert(e):
            for_range(vend_ref[e], pend_ref[e], lambda r: zero_row(r).start())
            for_range(vend_ref[e], pend_ref[e], lambda r: zero_row(r).wait())

        for_range(0, n_exp, per_expert)
        tail_lo = pend_ref[n_exp - 1] // tm
        for_range(tail_lo, n_rows // tm, lambda c: zero_chunk(c).start())
        for_range(tail_lo, n_rows // tm, lambda c: zero_chunk(c).wait())


def _dispatch(h, norm_g, dest_flat, valid_end, pad_end, n_rows):
    n, d = h.shape
    tm = min(ROW_TM, n)
    assert EXP_TM % tm == 0
    nsteps = n // tm
    grid_spec = pltpu.PrefetchScalarGridSpec(
        num_scalar_prefetch=2,
        grid=(nsteps,),
        in_specs=[
            pl.BlockSpec((tm, d), lambda i, *_: (i, 0)),
            pl.BlockSpec((1, d), lambda i, *_: (0, 0)),
            pl.BlockSpec((tm * TOP_K,), lambda i, *_: (i,), memory_space=pltpu.SMEM),
        ],
        out_specs=pl.BlockSpec(memory_space=pl.ANY),
        scratch_shapes=[
            pltpu.VMEM((2, tm, d), F32),
            pltpu.SemaphoreType.DMA((2,)),
            pltpu.SemaphoreType.DMA(()),
        ],
    )
    return pl.pallas_call(
        functools.partial(_dispatch_kernel, nsteps=nsteps),
        grid_spec=grid_spec,
        out_shape=jax.ShapeDtypeStruct((n_rows, d), F32),
        compiler_params=_cparams(("arbitrary",)),
        name="dispatch",
    )(valid_end, pad_end, h, norm_g.reshape(1, d), dest_flat)


def _expert_changed(be_ref, i):
    return (i == 0) | (be_ref[i] != be_ref[jnp.maximum(i - 1, 0)])


def _gate_up_kernel(be_ref, xb_ref, nu_ref, x_ref, wg_ref, wu_ref, bg_ref, bu_ref, o_ref,
                    wgb_ref, wub_ref):
    i = pl.program_id(1)

    @pl.when(_expert_changed(be_ref, i))
    def _():
        wgb_ref[...] = wg_ref[...].astype(BF16)
        wub_ref[...] = wu_ref[...].astype(BF16)

    @pl.when(i < nu_ref[0])
    def _():
        x = x_ref[...].astype(BF16)
        g = jnp.dot(x, wgb_ref[...], preferred_element_type=F32) + bg_ref[...]
        u = jnp.dot(x, wub_ref[...], preferred_element_type=F32) + bu_ref[...]
        g = jnp.minimum(g, SWIGLU_LIMIT)
        u = jnp.clip(u, -SWIGLU_LIMIT, SWIGLU_LIMIT)
        o_ref[...] = (g * jax.nn.sigmoid(SWIGLU_ALPHA * g) * (u + 1)).astype(o_ref.dtype)

    @pl.when(i >= nu_ref[0])
    def _():
        o_ref[...] = jnp.zeros_like(o_ref)


def _gate_up(xs, w_gu, b_gu, block_e, x_block, n_used):
    n_rows, d = xs.shape
    n_exp, _, f2 = w_gu.shape
    f = f2 // 2
    tm, tf = EXP_TM, GU_TF
    nj = f // tf
    b3 = b_gu.reshape(n_exp, 1, f2)
    grid_spec = pltpu.PrefetchScalarGridSpec(
        num_scalar_prefetch=3,
        grid=(nj, n_rows // tm),
        in_specs=[
            pl.BlockSpec((tm, d), lambda j, i, be, xb, nu: (xb[i], 0)),
            pl.BlockSpec((None, d, tf), lambda j, i, be, xb, nu: (be[i], 0, j)),
            pl.BlockSpec((None, d, tf), lambda j, i, be, xb, nu: (be[i], 0, j + nj)),
            pl.BlockSpec((None, 1, tf), lambda j, i, be, xb, nu: (be[i], 0, j)),
            pl.BlockSpec((None, 1, tf), lambda j, i, be, xb, nu: (be[i], 0, j + nj)),
        ],
        out_specs=pl.BlockSpec((tm, tf), lambda j, i, be, xb, nu: (i, j)),
        scratch_shapes=[pltpu.VMEM((d, tf), BF16), pltpu.VMEM((d, tf), BF16)],
    )
    return pl.pallas_call(
        _gate_up_kernel,
        grid_spec=grid_spec,
        out_shape=jax.ShapeDtypeStruct((n_rows, f), BF16),
        compiler_params=_cparams(("arbitrary", "arbitrary")),
        name="expert_gate_up",
    )(block_e, x_block, n_used, xs, w_gu, w_gu, b3, b3)


def _down_kernel(be_ref, xb_ref, nu_ref, a_ref, w_ref, b_ref, o_ref, wb_ref):
    i = pl.program_id(1)

    @pl.when(_expert_changed(be_ref, i))
    def _():
        wb_ref[...] = w_ref[...].astype(BF16)

    @pl.when(i < nu_ref[0])
    def _():
        o_ref[...] = jnp.dot(a_ref[...], wb_ref[...], preferred_element_type=F32) + b_ref[...]

    @pl.when(i >= nu_ref[0])
    def _():
        o_ref[...] = jnp.zeros_like(o_ref)


def _down(act, w_down, b_down, block_e, x_block, n_used):
    n_rows, f = act.shape
    n_exp, _, d = w_down.shape
    tm, tn = EXP_TM, DN_TN
    grid_spec = pltpu.PrefetchScalarGridSpec(
        num_scalar_prefetch=3,
        grid=(d // tn, n_rows // tm),
        in_specs=[
            pl.BlockSpec((tm, f), lambda j, i, be, xb, nu: (xb[i], 0)),
            pl.BlockSpec((None, f, tn), lambda j, i, be, xb, nu: (be[i], 0, j)),
            pl.BlockSpec((None, 1, tn), lambda j, i, be, xb, nu: (be[i], 0, j)),
        ],
        out_specs=pl.BlockSpec((tm, tn), lambda j, i, be, xb, nu: (i, j)),
        scratch_shapes=[pltpu.VMEM((f, tn), BF16)],
    )
    return pl.pallas_call(
        _down_kernel,
        grid_spec=grid_spec,
        out_shape=jax.ShapeDtypeStruct((n_rows, d), F32),
        compiler_params=_cparams(("arbitrary", "arbitrary")),
        name="expert_down",
    )(block_e, x_block, n_used, act, w_down, b_down.reshape(n_exp, 1, d))


def _combine_ple_kernel(dcur_ref, dnext_ref, h_ref, gate_ref, p_ref, wp_ref, png_ref, lng_ref,
                        wg_ref, bg_ref, fing_ref, y_hbm, o_ref, ybuf_ref, sems, *, final):
    tm = h_ref.shape[0]
    i = pl.program_id(0)
    nsteps = pl.num_programs(0)
    slot = i % 2

    def issue(dref, s):
        def body(t, carry):
            for j in range(TOP_K):
                r = dref[t * TOP_K + j]
                pltpu.make_async_copy(y_hbm.at[pl.ds(r, 1)], ybuf_ref.at[s, j, pl.ds(t, 1)],
                                      sems.at[s]).start()
            return carry

        lax.fori_loop(0, tm, body, 0)

    @pl.when(i == 0)
    def _():
        issue(dcur_ref, slot)

    @pl.when(i + 1 < nsteps)
    def _():
        issue(dnext_ref, 1 - slot)

    for j in range(TOP_K):
        pltpu.make_async_copy(y_hbm.at[pl.ds(0, tm)], ybuf_ref.at[slot, j], sems.at[slot]).wait()

    gates = gate_ref[...]
    moe = gates[:, 0:1] * ybuf_ref[slot, 0]
    for j in range(1, TOP_K):
        moe = moe + gates[:, j:j + 1] * ybuf_ref[slot, j]
    h2 = h_ref[...] + moe
    pe = _rms(jnp.dot(p_ref[...].astype(BF16), wp_ref[...], preferred_element_type=F32), png_ref[...])
    hn = _rms(h2, lng_ref[...]).astype(BF16)
    gate = jax.nn.sigmoid(jnp.dot(hn, wg_ref[...], preferred_element_type=F32) + bg_ref[...])
    out = h2 + gate * pe
    if final:
        out = _rms(out, fing_ref[...])
    o_ref[...] = out


def _combine_ple(h, y_sorted, dest_flat, gates_t, p, w_proj, proj_norm_g, norm_g, w_gate, b_gate,
                 final_g, final):
    n, d = h.shape
    pd = p.shape[1]
    tm = min(ROW_TM, n)
    nsteps = n // tm
    row = lambda i, *_: (i, 0)
    const = lambda i, *_: (0, 0)
    return pl.pallas_call(
        functools.partial(_combine_ple_kernel, final=final),
        grid=(nsteps,),
        in_specs=[
            pl.BlockSpec((tm * TOP_K,), lambda i: (i,), memory_space=pltpu.SMEM),
            pl.BlockSpec((tm * TOP_K,), lambda i: (jnp.minimum(i + 1, nsteps - 1),),
                         memory_space=pltpu.SMEM),
            pl.BlockSpec((tm, d), row),
            pl.BlockSpec((tm, TOP_K), row),
            pl.BlockSpec((tm, pd), row),
            pl.BlockSpec((pd, d), const),
            pl.BlockSpec((1, d), const),
            pl.BlockSpec((1, d), const),
            pl.BlockSpec((d, d), const),
            pl.BlockSpec((1, d), const),
            pl.BlockSpec((1, d), const),
            pl.BlockSpec(memory_space=pl.ANY),
        ],
        out_specs=pl.BlockSpec((tm, d), row),
        out_shape=jax.ShapeDtypeStruct((n, d), F32),
        scratch_shapes=[pltpu.VMEM((2, TOP_K, tm, d), F32), pltpu.SemaphoreType.DMA((2,))],
        compiler_params=_cparams(("arbitrary",)),
        name="combine_ple",
    )(dest_flat, dest_flat, h, gates_t, p, w_proj.astype(BF16), proj_norm_g.reshape(1, d),
      norm_g.reshape(1, d), w_gate.astype(BF16), b_gate.reshape(1, d), final_g.reshape(1, d), y_sorted)


def _moe_ple(h, ffn_g, router_w, router_b, w_gu, b_gu, w_down, b_down, p, w_proj, proj_norm_g,
             ple_g, w_gate, b_gate, final_g, final):
    n, d = h.shape
    n_exp = router_w.shape[1]
    top_idx, gates, rank, counts = _router(h, ffn_g, router_w, router_b)
    counts = counts[:, 0]
    padded = (counts + EXP_TM - 1) // EXP_TM * EXP_TM
    pad_end = jnp.cumsum(padded).astype(jnp.int32)
    pad_start = pad_end - padded
    valid_end = pad_start + counts
    n_rows = -(-(n * TOP_K + n_exp * (EXP_TM - 1)) // EXP_TM) * EXP_TM
    n_blocks = n_rows // EXP_TM
    dest = jnp.take(pad_start, top_idx) + rank
    dest_flat = jnp.transpose(dest).reshape(n * TOP_K)
    n_used = (pad_end[-1] // EXP_TM).astype(jnp.int32)
    x_block = jnp.minimum(jnp.arange(n_blocks, dtype=jnp.int32), n_used - 1)
    block_e = jnp.minimum(
        jnp.searchsorted(pad_end, x_block * EXP_TM, side='right'), n_exp - 1).astype(jnp.int32)
    n_used1 = n_used.reshape(1)

    xs = _dispatch(h, ffn_g, dest_flat, valid_end, pad_end, n_rows)
    act = _gate_up(xs, w_gu, b_gu, block_e, x_block, n_used1)
    ys = _down(act, w_down, b_down, block_e, x_block, n_used1)
    return _combine_ple(h, ys, dest_flat, jnp.transpose(gates), p, w_proj, proj_norm_g, ple_g,
                        w_gate, b_gate, final_g, final)


@jax.jit
def _forward(x, p, mix_norm, ffn_norm, ple_norm, final_norm,
             a_w_in, a_b_in, a_ln_g, a_ln_b, a_w_s, a_b_s, a_w_out, a_b_out,
             c_w_in, c_b_in, c_dw_w, c_dw_b, c_ln_g, c_ln_b, c_w_out, c_b_out,
             router_w, router_b, e_w_gu, e_b_gu, e_w_down, e_b_down,
             ple_w_proj, ple_proj_norm, ple_w_gate, ple_b_gate):
    bsz, seq, d = x.shape
    depth = p.shape[0]
    n = bsz * seq
    h = x.reshape(n, d)
    for i in range(depth):
        j = i // 2
        if i % 2 == 0:
            z = _in_proj_gelu(h, mix_norm[i], a_w_in[j], a_b_in[j])
            h = _sgu_out(z, a_ln_g[j], a_ln_b[j], a_w_s[j], a_b_s[j], a_w_out[j], a_b_out[j], h)
        else:
            z = _in_proj_glu(h, mix_norm[i], c_w_in[j], c_b_in[j])
            h = _conv_out(z, seq, c_dw_w[j], c_dw_b[j], c_ln_g[j], c_ln_b[j], c_w_out[j],
                          c_b_out[j], h)
        h = _moe_ple(h, ffn_norm[i], router_w[i], router_b[i], e_w_gu[i], e_b_gu[i], e_w_down[i],
                     e_b_down[i], p[i].reshape(n, -1), ple_w_proj[i], ple_proj_norm[i],
                     ple_norm[i], ple_w_gate[i], ple_b_gate[i], final_norm, i == depth - 1)
    return h.reshape(bsz, seq, d)


def kernel(x, p, mix_norm, ffn_norm, ple_norm, final_norm, a_w_in, a_b_in, a_ln_g, a_ln_b, a_w_s, a_b_s, a_w_out, a_b_out, c_w_in, c_b_in, c_dw_w, c_dw_b, c_ln_g, c_ln_b, c_w_out, c_b_out, router_w, router_b, e_w_gu, e_b_gu, e_w_down, e_b_down, ple_w_proj, ple_proj_norm, ple_w_gate, ple_b_gate):
    return _forward(x, p, mix_norm, ffn_norm, ple_norm, final_norm,
                    a_w_in, a_b_in, a_ln_g, a_ln_b, a_w_s, a_b_s, a_w_out, a_b_out,
                    c_w_in, c_b_in, c_dw_w, c_dw_b, c_ln_g, c_ln_b, c_w_out, c_b_out,
                    router_w, router_b, e_w_gu, e_b_gu, e_w_down, e_b_down,
                    ple_w_proj, ple_proj_norm, ple_w_gate, ple_b_gate)
```

```python
import functools

import jax
import jax.numpy as jnp
from jax import lax
from jax.experimental import pallas as pl
from jax.experimental.pallas import tpu as pltpu

CHUNK = 128
A_HEADS = 16
CONV_K = 31
TOP_K = 4
SWIGLU_LIMIT = 7.0
SWIGLU_ALPHA = 1.702
RMS_EPS = 1e-5
LN_EPS = 1e-5

LANES = 128
SUBLANES = 8
VMEM_LIMIT_BYTES = 56 * 1024 * 1024

IN_TM = 1024
IN_TN = 1024
MIX_TM = 256
CONV_HALO = 32
ROUTER_TM = 512
ROW_TM = 256
EXP_TM = 512
GU_TF = 1024
ISSUE_UNROLL = 4

F32 = jnp.float32
BF16 = jnp.bfloat16


def _cparams(sems):
    return pltpu.CompilerParams(dimension_semantics=sems, vmem_limit_bytes=VMEM_LIMIT_BYTES)


def _rms(x, g):
    return x * lax.rsqrt(jnp.mean(x * x, axis=-1, keepdims=True) + RMS_EPS) * g


def _layer_norm(x, g, b):
    xc = x - jnp.mean(x, axis=-1, keepdims=True)
    return xc * lax.rsqrt(jnp.mean(xc * xc, axis=-1, keepdims=True) + LN_EPS) * g + b


def _in_proj_gelu_kernel(h_ref, g_ref, w_ref, b_ref, o_ref, hn_ref):
    @pl.when(pl.program_id(1) == 0)
    def _():
        hn_ref[...] = _rms(h_ref[...], g_ref[...]).astype(BF16)

    z = jnp.dot(hn_ref[...], w_ref[...], preferred_element_type=F32) + b_ref[...]
    o_ref[...] = jax.nn.gelu(z).astype(o_ref.dtype)


def _in_proj_glu_kernel(h_ref, g_ref, wa_ref, wg_ref, ba_ref, bg_ref, o_ref, hn_ref):
    @pl.when(pl.program_id(1) == 0)
    def _():
        hn_ref[...] = _rms(h_ref[...], g_ref[...]).astype(BF16)

    hn = hn_ref[...]
    a = jnp.dot(hn, wa_ref[...], preferred_element_type=F32) + ba_ref[...]
    g = jnp.dot(hn, wg_ref[...], preferred_element_type=F32) + bg_ref[...]
    o_ref[...] = (a * jax.nn.sigmoid(g)).astype(o_ref.dtype)


def _in_proj_gelu(h, norm_g, w, b):
    n, d = h.shape
    cols = w.shape[1]
    tm, tn = min(IN_TM, n), IN_TN
    return pl.pallas_call(
        _in_proj_gelu_kernel,
        grid=(n // tm, cols // tn),
        in_specs=[
            pl.BlockSpec((tm, d), lambda i, j: (i, 0)),
            pl.BlockSpec((1, d), lambda i, j: (0, 0)),
            pl.BlockSpec((d, tn), lambda i, j: (0, j)),
            pl.BlockSpec((1, tn), lambda i, j: (0, j)),
        ],
        out_specs=pl.BlockSpec((tm, tn), lambda i, j: (i, j)),
        out_shape=jax.ShapeDtypeStruct((n, cols), BF16),
        scratch_shapes=[pltpu.VMEM((tm, d), BF16)],
        compiler_params=_cparams(("arbitrary", "arbitrary")),
        name="in_proj_gelu",
    )(h, norm_g.reshape(1, d), w.astype(BF16), b.reshape(1, cols))


def _in_proj_glu(h, norm_g, w, b):
    n, d = h.shape
    width = w.shape[1] // 2
    tm, tn = min(IN_TM, n), IN_TN
    nj = width // tn
    wb = w.astype(BF16)
    b2 = b.reshape(1, 2 * width)
    return pl.pallas_call(
        _in_proj_glu_kernel,
        grid=(n // tm, nj),
        in_specs=[
            pl.BlockSpec((tm, d), lambda i, j: (i, 0)),
            pl.BlockSpec((1, d), lambda i, j: (0, 0)),
            pl.BlockSpec((d, tn), lambda i, j: (0, j)),
            pl.BlockSpec((d, tn), lambda i, j: (0, j + nj)),
            pl.BlockSpec((1, tn), lambda i, j: (0, j)),
            pl.BlockSpec((1, tn), lambda i, j: (0, j + nj)),
        ],
        out_specs=pl.BlockSpec((tm, tn), lambda i, j: (i, j)),
        out_shape=jax.ShapeDtypeStruct((n, width), BF16),
        scratch_shapes=[pltpu.VMEM((tm, d), BF16)],
        compiler_params=_cparams(("arbitrary", "arbitrary")),
        name="in_proj_glu",
    )(h, norm_g.reshape(1, d), wb, wb, b2, b2)


def _sgu_out_kernel(u_ref, v_ref, lng_ref, lnb_ref, ws_ref, bsb_ref, wo_ref, bo_ref, h_ref,
                    o_ref, mixed_ref):
    tm = u_ref.shape[0]
    nchunk = tm // CHUNK
    vn = _layer_norm(v_ref[...].astype(F32), lng_ref[...], lnb_ref[...]).astype(BF16)
    tgt = lax.broadcasted_iota(jnp.int32, (CHUNK, CHUNK), 0)
    src = lax.broadcasted_iota(jnp.int32, (CHUNK, CHUNK), 1)
    causal = src <= tgt
    for hd in range(A_HEADS):
        hs = slice(hd * LANES, (hd + 1) * LANES)
        w = jnp.where(causal, ws_ref[hd], 0.0).astype(BF16)
        rhs = jnp.concatenate([vn[c * CHUNK:(c + 1) * CHUNK, hs] for c in range(nchunk)], axis=1)
        mix = jnp.dot(w, rhs, preferred_element_type=F32)
        for c in range(nchunk):
            mixed_ref[c * CHUNK:(c + 1) * CHUNK, hs] = mix[:, c * LANES:(c + 1) * LANES] + bsb_ref[:, hs]
    gated = (u_ref[...].astype(F32) * mixed_ref[...]).astype(BF16)
    o_ref[...] = h_ref[...] + (jnp.dot(gated, wo_ref[...], preferred_element_type=F32) + bo_ref[...])


def _sgu_out(z, ln_g, ln_b, w_s, b_s, w_out, b_out, h):
    n, d = h.shape
    width = z.shape[1] // 2
    assert width // A_HEADS == LANES
    tm = MIX_TM
    bsb = jnp.repeat(jnp.transpose(b_s), LANES, axis=1)
    return pl.pallas_call(
        _sgu_out_kernel,
        grid=(n // tm,),
        in_specs=[
            pl.BlockSpec((tm, width), lambda i: (i, 0)),
            pl.BlockSpec((tm, width), lambda i: (i, 1)),
            pl.BlockSpec((1, width), lambda i: (0, 0)),
            pl.BlockSpec((1, width), lambda i: (0, 0)),
            pl.BlockSpec((A_HEADS, CHUNK, CHUNK), lambda i: (0, 0, 0)),
            pl.BlockSpec((CHUNK, width), lambda i: (0, 0)),
            pl.BlockSpec((width, d), lambda i: (0, 0)),
            pl.BlockSpec((1, d), lambda i: (0, 0)),
            pl.BlockSpec((tm, d), lambda i: (i, 0)),
        ],
        out_specs=pl.BlockSpec((tm, d), lambda i: (i, 0)),
        out_shape=jax.ShapeDtypeStruct((n, d), F32),
        scratch_shapes=[pltpu.VMEM((tm, width), F32)],
        compiler_params=_cparams(("arbitrary",)),
        name="sgu_out",
    )(z, z, ln_g.reshape(1, width), ln_b.reshape(1, width), w_s, bsb, w_out.astype(BF16),
      b_out.reshape(1, d), h)


def _conv_out_kernel(z_ref, halo_ref, dww_ref, dwb_ref, lng_ref, lnb_ref, wo_ref, bo_ref, h_ref,
                     o_ref, zbuf_ref, acc_ref, *, tiles_per_seq):
    tm, width = z_ref.shape
    first = (pl.program_id(0) % tiles_per_seq) == 0
    zbuf_ref[0:CONV_HALO, :] = jnp.where(first, 0.0, halo_ref[...].astype(F32))
    zbuf_ref[CONV_HALO:CONV_HALO + tm, :] = z_ref[...].astype(F32)
    zbuf_ref[CONV_HALO + tm:CONV_HALO + tm + SUBLANES, :] = jnp.zeros((SUBLANES, width), F32)
    first_tap = CONV_HALO - (CONV_K - 1)

    def lane_tile(l, carry):
        ls = pl.ds(pl.multiple_of(l * LANES, LANES), LANES)
        out = None
        for r in range(SUBLANES):
            part = None
            for q in range((CONV_HALO + SUBLANES) // SUBLANES):
                k = q * SUBLANES + r - first_tap
                if 0 <= k < CONV_K:
                    term = dww_ref[k:k + 1, ls] * zbuf_ref[q * SUBLANES:q * SUBLANES + tm + SUBLANES, ls]
                    part = term if part is None else part + term
            shifted = part[r:r + tm, :]
            out = shifted if out is None else out + shifted
        acc_ref[:, ls] = out + dwb_ref[:, ls]
        return carry

    lax.fori_loop(0, width // LANES, lane_tile, 0)
    y = jax.nn.silu(_layer_norm(acc_ref[...], lng_ref[...], lnb_ref[...])).astype(BF16)
    o_ref[...] = h_ref[...] + (jnp.dot(y, wo_ref[...], preferred_element_type=F32) + bo_ref[...])


def _conv_out(z, seq, dw_w, dw_b, ln_g, ln_b, w_out, b_out, h):
    n, d = h.shape
    width = z.shape[1]
    tm = MIX_TM
    assert seq % tm == 0 and tm % CONV_HALO == 0 and CONV_HALO >= CONV_K - 1
    ratio = tm // CONV_HALO
    return pl.pallas_call(
        functools.partial(_conv_out_kernel, tiles_per_seq=seq // tm),
        grid=(n // tm,),
        in_specs=[
            pl.BlockSpec((tm, width), lambda i: (i, 0)),
            pl.BlockSpec((CONV_HALO, width), lambda i: (jnp.maximum(i * ratio - 1, 0), 0)),
            pl.BlockSpec((CONV_K, width), lambda i: (0, 0)),
            pl.BlockSpec((1, width), lambda i: (0, 0)),
            pl.BlockSpec((1, width), lambda i: (0, 0)),
            pl.BlockSpec((1, width), lambda i: (0, 0)),
            pl.BlockSpec((width, d), lambda i: (0, 0)),
            pl.BlockSpec((1, d), lambda i: (0, 0)),
            pl.BlockSpec((tm, d), lambda i: (i, 0)),
        ],
        out_specs=pl.BlockSpec((tm, d), lambda i: (i, 0)),
        out_shape=jax.ShapeDtypeStruct((n, d), F32),
        scratch_shapes=[pltpu.VMEM((CONV_HALO + tm + SUBLANES, width), F32),
                        pltpu.VMEM((tm, width), F32)],
        compiler_params=_cparams(("arbitrary",)),
        name="conv_out",
    )(z, z, dw_w, dw_b.reshape(1, width), ln_g.reshape(1, width), ln_b.reshape(1, width),
      w_out.astype(BF16), b_out.reshape(1, d), h)


def _router_kernel(h_ref, g_ref, rwt_ref, rb_ref, idx_ref, gate_ref, rank_ref, cnt_ref, carry_ref):
    tm = h_ref.shape[0]
    n_exp = rwt_ref.shape[0]

    @pl.when(pl.program_id(0) == 0)
    def _():
        carry_ref[...] = jnp.zeros_like(carry_ref)

    hn = _rms(h_ref[...], g_ref[...])
    logits = lax.dot_general(rwt_ref[...], hn, (((1,), (1,)), ((), ())),
                             precision=lax.Precision.HIGHEST,
                             preferred_element_type=F32) + rb_ref[...]
    eio = lax.broadcasted_iota(jnp.int32, (n_exp, tm), 0)
    cur = logits
    sels, tops = [], []
    for j in range(TOP_K):
        m = jnp.max(cur, axis=0, keepdims=True)
        idx = jnp.min(jnp.where(cur == m, eio, n_exp), axis=0, keepdims=True)
        sel = eio == idx
        idx_ref[j:j + 1, :] = idx
        sels.append(sel)
        tops.append(m)
        cur = jnp.where(sel, -jnp.inf, cur)
    exps = [jnp.exp(t - tops[0]) for t in tops]
    denom = exps[0] + exps[1] + exps[2] + exps[3]
    for j in range(TOP_K):
        gate_ref[j:j + 1, :] = exps[j] / denom
    onehot = jnp.zeros((n_exp, tm), F32)
    for sel in sels:
        onehot = onehot + sel.astype(F32)
    before = (lax.broadcasted_iota(jnp.int32, (tm, tm), 0)
              < lax.broadcasted_iota(jnp.int32, (tm, tm), 1)).astype(BF16)
    prefix = jnp.dot(onehot.astype(BF16), before, preferred_element_type=F32)
    carry = carry_ref[...]
    rank_all = prefix + jnp.concatenate([carry] * (tm // LANES), axis=1)
    for j in range(TOP_K):
        rank_ref[j:j + 1, :] = jnp.sum(jnp.where(sels[j], rank_all, 0.0), axis=0,
                                       keepdims=True).astype(jnp.int32)
    carry = carry + jnp.sum(onehot, axis=1, keepdims=True)
    carry_ref[...] = carry
    cnt_ref[...] = carry.astype(jnp.int32)


def _router(h, norm_g, router_w, router_b):
    n, d = h.shape
    n_exp = router_w.shape[1]
    tm = min(ROUTER_TM, n)
    out_spec = pl.BlockSpec((TOP_K, tm), lambda i: (0, i))
    return pl.pallas_call(
        _router_kernel,
        grid=(n // tm,),
        in_specs=[
            pl.BlockSpec((tm, d), lambda i: (i, 0)),
            pl.BlockSpec((1, d), lambda i: (0, 0)),
            pl.BlockSpec((n_exp, d), lambda i: (0, 0)),
            pl.BlockSpec((n_exp, 1), lambda i: (0, 0)),
        ],
        out_specs=[out_spec, out_spec, out_spec, pl.BlockSpec((n_exp, LANES), lambda i: (0, 0))],
        out_shape=[
            jax.ShapeDtypeStruct((TOP_K, n), jnp.int32),
            jax.ShapeDtypeStruct((TOP_K, n), F32),
            jax.ShapeDtypeStruct((TOP_K, n), jnp.int32),
            jax.ShapeDtypeStruct((n_exp, LANES), jnp.int32),
        ],
        scratch_shapes=[pltpu.VMEM((n_exp, LANES), F32)],
        compiler_params=_cparams(("arbitrary",)),
        name="router",
    )(h, norm_g.reshape(1, d), jnp.transpose(router_w), router_b.reshape(n_exp, 1))


def _dispatch_kernel(vend_ref, pend_ref, h_ref, g_ref, dest_ref, xs_hbm, hn_ref, sems, zsem, *, nsteps):
    tm = h_ref.shape[0]
    n_exp = vend_ref.shape[0]
    n_rows = xs_hbm.shape[0]
    i = pl.program_id(0)
    slot = i % 2

    def wait_slot(s):
        for _ in range(TOP_K):
            pltpu.make_async_copy(hn_ref.at[s], xs_hbm.at[pl.ds(0, tm)], sems.at[s]).wait()

    @pl.when(i >= 2)
    def _():
        wait_slot(slot)

    hn_ref[slot] = _rms(h_ref[...], g_ref[...])

    def issue(t, carry):
        for j in range(TOP_K):
            d = dest_ref[t * TOP_K + j]
            pltpu.make_async_copy(hn_ref.at[slot, pl.ds(t, 1)], xs_hbm.at[pl.ds(d, 1)],
                                  sems.at[slot]).start()
        return carry

    lax.fori_loop(0, tm, issue, 0, unroll=ISSUE_UNROLL)

    @pl.when(i == nsteps - 1)
    def _():
        wait_slot(slot)
        if nsteps > 1:
            wait_slot(1 - slot)
        hn_ref[0] = jnp.zeros(hn_ref.shape[1:], F32)

        def zero_row(r):
            return pltpu.make_async_copy(hn_ref.at[0, pl.ds(0, 1)], xs_hbm.at[pl.ds(r, 1)], zsem)

        def zero_chunk(c):
            return pltpu.make_async_copy(hn_ref.at[0], xs_hbm.at[pl.ds(c * tm, tm)], zsem)

        def for_range(lo, hi, fn):
            def body(r, carry):
                fn(r)
                return carry

            lax.fori_loop(lo, hi, body, 0)

        def per_expert(e):
            for_range(vend_ref[e], pend_ref[e], lambda r: zero_row(r).start())
            for_range(vend_ref[e], pend_ref[e], lambda r: zero_row(r).wait())

        for_range(0, n_exp, per_expert)
        tail_lo = pend_ref[n_exp - 1] // tm
        for_range(tail_lo, n_rows // tm, lambda c: zero_chunk(c).start())
        for_range(tail_lo, n_rows // tm, lambda c: zero_chunk(c).wait())


def _dispatch(h, norm_g, dest_flat, valid_end, pad_end, n_rows):
    n, d = h.shape
    tm = min(ROW_TM, n)
    assert EXP_TM % tm == 0
    nsteps = n // tm
    grid_spec = pltpu.PrefetchScalarGridSpec(
        num_scalar_prefetch=2,
        grid=(nsteps,),
        in_specs=[
            pl.BlockSpec((tm, d), lambda i, *_: (i, 0)),
            pl.BlockSpec((1, d), lambda i, *_: (0, 0)),
            pl.BlockSpec((tm * TOP_K,), lambda i, *_: (i,), memory_space=pltpu.SMEM),
        ],
        out_specs=pl.BlockSpec(memory_space=pl.ANY),
        scratch_shapes=[
            pltpu.VMEM((2, tm, d), F32),
            pltpu.SemaphoreType.DMA((2,)),
            pltpu.SemaphoreType.DMA(()),
        ],
    )
    return pl.pallas_call(
        functools.partial(_dispatch_kernel, nsteps=nsteps),
        grid_spec=grid_spec,
        out_shape=jax.ShapeDtypeStruct((n_rows, d), F32),
        compiler_params=_cparams(("arbitrary",)),
        name="dispatch",
    )(valid_end, pad_end, h, norm_g.reshape(1, d), dest_flat)


def _gate_up_kernel(be_ref, xb_ref, nu_ref, x_ref, wg_ref, wu_ref, bg_ref, bu_ref, o_ref):
    i = pl.program_id(1)

    @pl.when(i < nu_ref[0])
    def _():
        x = x_ref[...]
        g = jnp.dot(x, wg_ref[...], preferred_element_type=F32) + bg_ref[...]
        u = jnp.dot(x, wu_ref[...], preferred_element_type=F32) + bu_ref[...]
        g = jnp.minimum(g, SWIGLU_LIMIT)
        u = jnp.clip(u, -SWIGLU_LIMIT, SWIGLU_LIMIT)
        o_ref[...] = g * jax.nn.sigmoid(SWIGLU_ALPHA * g) * (u + 1)

    @pl.when(i >= nu_ref[0])
    def _():
        o_ref[...] = jnp.zeros_like(o_ref)


def _gate_up(xs, w_gu, b_gu, layer, block_e, x_block, n_used):
    n_rows, d = xs.shape
    depth, n_exp, _, f2 = w_gu.shape
    f = f2 // 2
    tm, tf = EXP_TM, GU_TF
    nj = f // tf
    b4 = b_gu.reshape(depth, n_exp, 1, f2)
    grid_spec = pltpu.PrefetchScalarGridSpec(
        num_scalar_prefetch=3,
        grid=(nj, n_rows // tm),
        in_specs=[
            pl.BlockSpec((tm, d), lambda j, i, be, xb, nu: (xb[i], 0)),
            pl.BlockSpec((None, None, d, tf), lambda j, i, be, xb, nu: (layer, be[i], 0, j)),
            pl.BlockSpec((None, None, d, tf), lambda j, i, be, xb, nu: (layer, be[i], 0, j + nj)),
            pl.BlockSpec((None, None, 1, tf), lambda j, i, be, xb, nu: (layer, be[i], 0, j)),
            pl.BlockSpec((None, None, 1, tf), lambda j, i, be, xb, nu: (layer, be[i], 0, j + nj)),
        ],
        out_specs=pl.BlockSpec((tm, tf), lambda j, i, be, xb, nu: (i, j)),
    )
    return pl.pallas_call(
        _gate_up_kernel,
        grid_spec=grid_spec,
        out_shape=jax.ShapeDtypeStruct((n_rows, f), F32),
        compiler_params=_cparams(("arbitrary", "arbitrary")),
        name="expert_gate_up",
    )(block_e, x_block, n_used, xs, w_gu, w_gu, b4, b4)


def _down_kernel(be_ref, xb_ref, nu_ref, a_ref, w_ref, b_ref, o_ref):
    i = pl.program_id(0)

    @pl.when(i < nu_ref[0])
    def _():
        o_ref[...] = jnp.dot(a_ref[...], w_ref[...], preferred_element_type=F32) + b_ref[...]

    @pl.when(i >= nu_ref[0])
    def _():
        o_ref[...] = jnp.zeros_like(o_ref)


def _down(act, w_down, b_down, layer, block_e, x_block, n_used):
    n_rows, f = act.shape
    depth, n_exp, _, d = w_down.shape
    tm = EXP_TM
    grid_spec = pltpu.PrefetchScalarGridSpec(
        num_scalar_prefetch=3,
        grid=(n_rows // tm,),
        in_specs=[
            pl.BlockSpec((tm, f), lambda i, be, xb, nu: (xb[i], 0)),
            pl.BlockSpec((None, None, f, d), lambda i, be, xb, nu: (layer, be[i], 0, 0)),
            pl.BlockSpec((None, None, 1, d), lambda i, be, xb, nu: (layer, be[i], 0, 0)),
        ],
        out_specs=pl.BlockSpec((tm, d), lambda i, be, xb, nu: (i, 0)),
    )
    return pl.pallas_call(
        _down_kernel,
        grid_spec=grid_spec,
        out_shape=jax.ShapeDtypeStruct((n_rows, d), F32),
        compiler_params=_cparams(("arbitrary",)),
        name="expert_down",
    )(block_e, x_block, n_used, act, w_down, b_down.reshape(depth, n_exp, 1, d))


def _combine_ple_kernel(dcur_ref, dnext_ref, h_ref, gate_ref, p_ref, wp_ref, png_ref, lng_ref,
                        wg_ref, bg_ref, fing_ref, y_hbm, o_ref, ybuf_ref, sems, *, final):
    tm = h_ref.shape[0]
    i = pl.program_id(0)
    nsteps = pl.num_programs(0)
    slot = i % 2

    def issue(dref, s):
        def body(t, carry):
            for j in range(TOP_K):
                r = dref[t * TOP_K + j]
                pltpu.make_async_copy(y_hbm.at[pl.ds(r, 1)], ybuf_ref.at[s, j, pl.ds(t, 1)],
                                      sems.at[s]).start()
            return carry

        lax.fori_loop(0, tm, body, 0, unroll=ISSUE_UNROLL)

    @pl.when(i == 0)
    def _():
        issue(dcur_ref, slot)

    @pl.when(i + 1 < nsteps)
    def _():
        issue(dnext_ref, 1 - slot)

    for j in range(TOP_K):
        pltpu.make_async_copy(y_hbm.at[pl.ds(0, tm)], ybuf_ref.at[slot, j], sems.at[slot]).wait()

    gates = gate_ref[...]
    moe = gates[:, 0:1] * ybuf_ref[slot, 0]
    for j in range(1, TOP_K):
        moe = moe + gates[:, j:j + 1] * ybuf_ref[slot, j]
    h2 = h_ref[...] + moe
    pe = _rms(jnp.dot(p_ref[...].astype(BF16), wp_ref[...], preferred_element_type=F32), png_ref[...])
    hn = _rms(h2, lng_ref[...]).astype(BF16)
    gate = jax.nn.sigmoid(jnp.dot(hn, wg_ref[...], preferred_element_type=F32) + bg_ref[...])
    out = h2 + gate * pe
    if final:
        out = _rms(out, fing_ref[...])
    o_ref[...] = out


def _combine_ple(h, y_sorted, dest_flat, gates_t, p, w_proj, proj_norm_g, norm_g, w_gate, b_gate,
                 final_g, final):
    n, d = h.shape
    pd = p.shape[1]
    tm = min(ROW_TM, n)
    nsteps = n // tm
    row = lambda i, *_: (i, 0)
    const = lambda i, *_: (0, 0)
    return pl.pallas_call(
        functools.partial(_combine_ple_kernel, final=final),
        grid=(nsteps,),
        in_specs=[
            pl.BlockSpec((tm * TOP_K,), lambda i: (i,), memory_space=pltpu.SMEM),
            pl.BlockSpec((tm * TOP_K,), lambda i: (jnp.minimum(i + 1, nsteps - 1),),
                         memory_space=pltpu.SMEM),
            pl.BlockSpec((tm, d), row),
            pl.BlockSpec((tm, TOP_K), row),
            pl.BlockSpec((tm, pd), row),
            pl.BlockSpec((pd, d), const),
            pl.BlockSpec((1, d), const),
            pl.BlockSpec((1, d), const),
            pl.BlockSpec((d, d), const),
            pl.BlockSpec((1, d), const),
            pl.BlockSpec((1, d), const),
            pl.BlockSpec(memory_space=pl.ANY),
        ],
        out_specs=pl.BlockSpec((tm, d), row),
        out_shape=jax.ShapeDtypeStruct((n, d), F32),
        scratch_shapes=[pltpu.VMEM((2, TOP_K, tm, d), F32), pltpu.SemaphoreType.DMA((2,))],
        compiler_params=_cparams(("arbitrary",)),
        name="combine_ple",
    )(dest_flat, dest_flat, h, gates_t, p, w_proj.astype(BF16), proj_norm_g.reshape(1, d),
      norm_g.reshape(1, d), w_gate.astype(BF16), b_gate.reshape(1, d), final_g.reshape(1, d), y_sorted)


def _moe_ple(h, layer, ffn_g, router_w, router_b, w_gu, b_gu, w_down, b_down, p, w_proj,
             proj_norm_g, ple_g, w_gate, b_gate, final_g, final):
    n, d = h.shape
    n_exp = router_w.shape[1]
    top_idx, gates, rank, counts = _router(h, ffn_g, router_w, router_b)
    counts = counts[:, 0]
    padded = (counts + EXP_TM - 1) // EXP_TM * EXP_TM
    pad_end = jnp.cumsum(padded).astype(jnp.int32)
    pad_start = pad_end - padded
    valid_end = pad_start + counts
    n_rows = -(-(n * TOP_K + n_exp * (EXP_TM - 1)) // EXP_TM) * EXP_TM
    n_blocks = n_rows // EXP_TM
    experts = jnp.arange(n_exp, dtype=jnp.int32)
    start_of = jnp.sum(jnp.where(top_idx[..., None] == experts, pad_start, 0), axis=-1)
    dest_flat = jnp.transpose(start_of + rank).reshape(n * TOP_K)
    n_used = (pad_end[-1] // EXP_TM).astype(jnp.int32)
    x_block = jnp.minimum(jnp.arange(n_blocks, dtype=jnp.int32), n_used - 1)
    block_e = jnp.minimum(
        jnp.sum((pad_end[None, :] <= (x_block * EXP_TM)[:, None]).astype(jnp.int32), axis=1),
        n_exp - 1)
    n_used1 = n_used.reshape(1)

    xs = _dispatch(h, ffn_g, dest_flat, valid_end, pad_end, n_rows)
    act = _gate_up(xs, w_gu, b_gu, layer, block_e, x_block, n_used1)
    ys = _down(act, w_down, b_down, layer, block_e, x_block, n_used1)
    return _combine_ple(h, ys, dest_flat, jnp.transpose(gates), p, w_proj, proj_norm_g, ple_g,
                        w_gate, b_gate, final_g, final)


@jax.jit
def _forward(x, p, mix_norm, ffn_norm, ple_norm, final_norm,
             a_w_in, a_b_in, a_ln_g, a_ln_b, a_w_s, a_b_s, a_w_out, a_b_out,
             c_w_in, c_b_in, c_dw_w, c_dw_b, c_ln_g, c_ln_b, c_w_out, c_b_out,
             router_w, router_b, e_w_gu, e_b_gu, e_w_down, e_b_down,
             ple_w_proj, ple_proj_norm, ple_w_gate, ple_b_gate):
    bsz, seq, d = x.shape
    depth = p.shape[0]
    n = bsz * seq
    h = x.reshape(n, d)
    for i in range(depth):
        j = i // 2
        if i % 2 == 0:
            z = _in_proj_gelu(h, mix_norm[i], a_w_in[j], a_b_in[j])
            h = _sgu_out(z, a_ln_g[j], a_ln_b[j], a_w_s[j], a_b_s[j], a_w_out[j], a_b_out[j], h)
        else:
            z = _in_proj_glu(h, mix_norm[i], c_w_in[j], c_b_in[j])
            h = _conv_out(z, seq, c_dw_w[j], c_dw_b[j], c_ln_g[j], c_ln_b[j], c_w_out[j],
                          c_b_out[j], h)
        h = _moe_ple(h, i, ffn_norm[i], router_w[i], router_b[i], e_w_gu, e_b_gu, e_w_down,
                     e_b_down, p[i].reshape(n, -1), ple_w_proj[i], ple_proj_norm[i],
                     ple_norm[i], ple_w_gate[i], ple_b_gate[i], final_norm, i == depth - 1)
    return h.reshape(bsz, seq, d)


def kernel(x, p, mix_norm, ffn_norm, ple_norm, final_norm, a_w_in, a_b_in, a_ln_g, a_ln_b, a_w_s, a_b_s, a_w_out, a_b_out, c_w_in, c_b_in, c_dw_w, c_dw_b, c_ln_g, c_ln_b, c_w_out, c_b_out, router_w, router_b, e_w_gu, e_b_gu, e_w_down, e_b_down, ple_w_proj, ple_proj_norm, ple_w_gate, ple_b_gate):
    return _forward(x, p, mix_norm, ffn_norm, ple_norm, final_norm,
                    a_w_in, a_b_in, a_ln_g, a_ln_b, a_w_s, a_b_s, a_w_out, a_b_out,
                    c_w_in, c_b_in, c_dw_w, c_dw_b, c_ln_g, c_ln_b, c_w_out, c_b_out,
                    router_w, router_b, e_w_gu, e_b_gu, e_w_down, e_b_down,
                    ple_w_proj, ple_proj_norm, ple_w_gate, ple_b_gate)
```

```python
import functools

import jax
import jax.numpy as jnp
from jax import lax
from jax.experimental import pallas as pl
from jax.experimental.pallas import tpu as pltpu

CHUNK = 128
A_HEADS = 16
CONV_K = 31
TOP_K = 4
SWIGLU_LIMIT = 7.0
SWIGLU_ALPHA = 1.702
RMS_EPS = 1e-5
LN_EPS = 1e-5

LANES = 128
SUBLANES = 8
VMEM_LIMIT_BYTES = 56 * 1024 * 1024

IN_TM = 1024
IN_TN = 1024
MIX_TM = 256
CONV_HALO = 32
ROUTER_TM = 512
ROW_TM = 256
EXP_TM = 512
GU_TF = 1024
ISSUE_UNROLL = 4
COMBINE_GROUPS = 4

F32 = jnp.float32
BF16 = jnp.bfloat16


def _cparams(sems):
    return pltpu.CompilerParams(dimension_semantics=sems, vmem_limit_bytes=VMEM_LIMIT_BYTES)


def _rms(x, g):
    return x * lax.rsqrt(jnp.mean(x * x, axis=-1, keepdims=True) + RMS_EPS) * g


def _layer_norm(x, g, b):
    xc = x - jnp.mean(x, axis=-1, keepdims=True)
    return xc * lax.rsqrt(jnp.mean(xc * xc, axis=-1, keepdims=True) + LN_EPS) * g + b


def _in_proj_gelu_kernel(h_ref, g_ref, w_ref, b_ref, o_ref, hn_ref):
    @pl.when(pl.program_id(1) == 0)
    def _():
        hn_ref[...] = _rms(h_ref[...], g_ref[...]).astype(BF16)

    z = jnp.dot(hn_ref[...], w_ref[...], preferred_element_type=F32) + b_ref[...]
    o_ref[...] = jax.nn.gelu(z).astype(o_ref.dtype)


def _in_proj_glu_kernel(h_ref, g_ref, wa_ref, wg_ref, ba_ref, bg_ref, o_ref, hn_ref):
    @pl.when(pl.program_id(1) == 0)
    def _():
        hn_ref[...] = _rms(h_ref[...], g_ref[...]).astype(BF16)

    hn = hn_ref[...]
    a = jnp.dot(hn, wa_ref[...], preferred_element_type=F32) + ba_ref[...]
    g = jnp.dot(hn, wg_ref[...], preferred_element_type=F32) + bg_ref[...]
    o_ref[...] = (a * jax.nn.sigmoid(g)).astype(o_ref.dtype)


def _in_proj_gelu(h, norm_g, w, b):
    n, d = h.shape
    cols = w.shape[1]
    tm, tn = min(IN_TM, n), IN_TN
    return pl.pallas_call(
        _in_proj_gelu_kernel,
        grid=(n // tm, cols // tn),
        in_specs=[
            pl.BlockSpec((tm, d), lambda i, j: (i, 0)),
            pl.BlockSpec((1, d), lambda i, j: (0, 0)),
            pl.BlockSpec((d, tn), lambda i, j: (0, j)),
            pl.BlockSpec((1, tn), lambda i, j: (0, j)),
        ],
        out_specs=pl.BlockSpec((tm, tn), lambda i, j: (i, j)),
        out_shape=jax.ShapeDtypeStruct((n, cols), BF16),
        scratch_shapes=[pltpu.VMEM((tm, d), BF16)],
        compiler_params=_cparams(("arbitrary", "arbitrary")),
        name="in_proj_gelu",
    )(h, norm_g.reshape(1, d), w.astype(BF16), b.reshape(1, cols))


def _in_proj_glu(h, norm_g, w, b):
    n, d = h.shape
    width = w.shape[1] // 2
    tm, tn = min(IN_TM, n), IN_TN
    nj = width // tn
    wb = w.astype(BF16)
    b2 = b.reshape(1, 2 * width)
    return pl.pallas_call(
        _in_proj_glu_kernel,
        grid=(n // tm, nj),
        in_specs=[
            pl.BlockSpec((tm, d), lambda i, j: (i, 0)),
            pl.BlockSpec((1, d), lambda i, j: (0, 0)),
            pl.BlockSpec((d, tn), lambda i, j: (0, j)),
            pl.BlockSpec((d, tn), lambda i, j: (0, j + nj)),
            pl.BlockSpec((1, tn), lambda i, j: (0, j)),
            pl.BlockSpec((1, tn), lambda i, j: (0, j + nj)),
        ],
        out_specs=pl.BlockSpec((tm, tn), lambda i, j: (i, j)),
        out_shape=jax.ShapeDtypeStruct((n, width), BF16),
        scratch_shapes=[pltpu.VMEM((tm, d), BF16)],
        compiler_params=_cparams(("arbitrary", "arbitrary")),
        name="in_proj_glu",
    )(h, norm_g.reshape(1, d), wb, wb, b2, b2)


def _sgu_out_kernel(u_ref, v_ref, lng_ref, lnb_ref, ws_ref, bsb_ref, wo_ref, bo_ref, h_ref,
                    o_ref, mixed_ref):
    tm = u_ref.shape[0]
    nchunk = tm // CHUNK
    vn = _layer_norm(v_ref[...].astype(F32), lng_ref[...], lnb_ref[...]).astype(BF16)
    tgt = lax.broadcasted_iota(jnp.int32, (CHUNK, CHUNK), 0)
    src = lax.broadcasted_iota(jnp.int32, (CHUNK, CHUNK), 1)
    causal = src <= tgt
    for hd in range(A_HEADS):
        hs = slice(hd * LANES, (hd + 1) * LANES)
        w = jnp.where(causal, ws_ref[hd], 0.0).astype(BF16)
        rhs = jnp.concatenate([vn[c * CHUNK:(c + 1) * CHUNK, hs] for c in range(nchunk)], axis=1)
        mix = jnp.dot(w, rhs, preferred_element_type=F32)
        for c in range(nchunk):
            mixed_ref[c * CHUNK:(c + 1) * CHUNK, hs] = mix[:, c * LANES:(c + 1) * LANES] + bsb_ref[:, hs]
    gated = (u_ref[...].astype(F32) * mixed_ref[...]).astype(BF16)
    o_ref[...] = h_ref[...] + (jnp.dot(gated, wo_ref[...], preferred_element_type=F32) + bo_ref[...])


def _sgu_out(z, ln_g, ln_b, w_s, b_s, w_out, b_out, h):
    n, d = h.shape
    width = z.shape[1] // 2
    assert width // A_HEADS == LANES
    tm = MIX_TM
    bsb = jnp.repeat(jnp.transpose(b_s), LANES, axis=1)
    return pl.pallas_call(
        _sgu_out_kernel,
        grid=(n // tm,),
        in_specs=[
            pl.BlockSpec((tm, width), lambda i: (i, 0)),
            pl.BlockSpec((tm, width), lambda i: (i, 1)),
            pl.BlockSpec((1, width), lambda i: (0, 0)),
            pl.BlockSpec((1, width), lambda i: (0, 0)),
            pl.BlockSpec((A_HEADS, CHUNK, CHUNK), lambda i: (0, 0, 0)),
            pl.BlockSpec((CHUNK, width), lambda i: (0, 0)),
            pl.BlockSpec((width, d), lambda i: (0, 0)),
            pl.BlockSpec((1, d), lambda i: (0, 0)),
            pl.BlockSpec((tm, d), lambda i: (i, 0)),
        ],
        out_specs=pl.BlockSpec((tm, d), lambda i: (i, 0)),
        out_shape=jax.ShapeDtypeStruct((n, d), F32),
        scratch_shapes=[pltpu.VMEM((tm, width), F32)],
        compiler_params=_cparams(("arbitrary",)),
        name="sgu_out",
    )(z, z, ln_g.reshape(1, width), ln_b.reshape(1, width), w_s, bsb, w_out.astype(BF16),
      b_out.reshape(1, d), h)


def _conv_out_kernel(z_ref, halo_ref, dww_ref, dwb_ref, lng_ref, lnb_ref, wo_ref, bo_ref, h_ref,
                     o_ref, zbuf_ref, acc_ref, *, tiles_per_seq):
    tm, width = z_ref.shape
    first = (pl.program_id(0) % tiles_per_seq) == 0
    zbuf_ref[0:CONV_HALO, :] = jnp.where(first, 0.0, halo_ref[...].astype(F32))
    zbuf_ref[CONV_HALO:CONV_HALO + tm, :] = z_ref[...].astype(F32)
    zbuf_ref[CONV_HALO + tm:CONV_HALO + tm + SUBLANES, :] = jnp.zeros((SUBLANES, width), F32)
    first_tap = CONV_HALO - (CONV_K - 1)

    def lane_tile(l, carry):
        ls = pl.ds(pl.multiple_of(l * LANES, LANES), LANES)
        out = None
        for r in range(SUBLANES):
            part = None
            for q in range((CONV_HALO + SUBLANES) // SUBLANES):
                k = q * SUBLANES + r - first_tap
                if 0 <= k < CONV_K:
                    term = dww_ref[k:k + 1, ls] * zbuf_ref[q * SUBLANES:q * SUBLANES + tm + SUBLANES, ls]
                    part = term if part is None else part + term
            shifted = part[r:r + tm, :]
            out = shifted if out is None else out + shifted
        acc_ref[:, ls] = out + dwb_ref[:, ls]
        return carry

    lax.fori_loop(0, width // LANES, lane_tile, 0)
    y = jax.nn.silu(_layer_norm(acc_ref[...], lng_ref[...], lnb_ref[...])).astype(BF16)
    o_ref[...] = h_ref[...] + (jnp.dot(y, wo_ref[...], preferred_element_type=F32) + bo_ref[...])


def _conv_out(z, seq, dw_w, dw_b, ln_g, ln_b, w_out, b_out, h):
    n, d = h.shape
    width = z.shape[1]
    tm = MIX_TM
    assert seq % tm == 0 and tm % CONV_HALO == 0 and CONV_HALO >= CONV_K - 1
    ratio = tm // CONV_HALO
    return pl.pallas_call(
        functools.partial(_conv_out_kernel, tiles_per_seq=seq // tm),
        grid=(n // tm,),
        in_specs=[
            pl.BlockSpec((tm, width), lambda i: (i, 0)),
            pl.BlockSpec((CONV_HALO, width), lambda i: (jnp.maximum(i * ratio - 1, 0), 0)),
            pl.BlockSpec((CONV_K, width), lambda i: (0, 0)),
            pl.BlockSpec((1, width), lambda i: (0, 0)),
            pl.BlockSpec((1, width), lambda i: (0, 0)),
            pl.BlockSpec((1, width), lambda i: (0, 0)),
            pl.BlockSpec((width, d), lambda i: (0, 0)),
            pl.BlockSpec((1, d), lambda i: (0, 0)),
            pl.BlockSpec((tm, d), lambda i: (i, 0)),
        ],
        out_specs=pl.BlockSpec((tm, d), lambda i: (i, 0)),
        out_shape=jax.ShapeDtypeStruct((n, d), F32),
        scratch_shapes=[pltpu.VMEM((CONV_HALO + tm + SUBLANES, width), F32),
                        pltpu.VMEM((tm, width), F32)],
        compiler_params=_cparams(("arbitrary",)),
        name="conv_out",
    )(z, z, dw_w, dw_b.reshape(1, width), ln_g.reshape(1, width), ln_b.reshape(1, width),
      w_out.astype(BF16), b_out.reshape(1, d), h)


def _router_kernel(h_ref, g_ref, rwt_ref, rb_ref, idx_ref, gate_ref, rank_ref, cnt_ref, carry_ref):
    tm = h_ref.shape[0]
    n_exp = rwt_ref.shape[0]

    @pl.when(pl.program_id(0) == 0)
    def _():
        carry_ref[...] = jnp.zeros_like(carry_ref)

    hn = _rms(h_ref[...], g_ref[...])
    logits = lax.dot_general(rwt_ref[...], hn, (((1,), (1,)), ((), ())),
                             preferred_element_type=F32) + rb_ref[...]
    eio = lax.broadcasted_iota(jnp.int32, (n_exp, tm), 0)
    cur = logits
    sels, tops = [], []
    for j in range(TOP_K):
        m = jnp.max(cur, axis=0, keepdims=True)
        idx = jnp.min(jnp.where(cur == m, eio, n_exp), axis=0, keepdims=True)
        sel = eio == idx
        idx_ref[j:j + 1, :] = idx
        sels.append(sel)
        tops.append(m)
        cur = jnp.where(sel, -jnp.inf, cur)
    exps = [jnp.exp(t - tops[0]) for t in tops]
    denom = exps[0] + exps[1] + exps[2] + exps[3]
    for j in range(TOP_K):
        gate_ref[j:j + 1, :] = exps[j] / denom
    onehot = jnp.zeros((n_exp, tm), F32)
    for sel in sels:
        onehot = onehot + sel.astype(F32)
    before = (lax.broadcasted_iota(jnp.int32, (tm, tm), 0)
              < lax.broadcasted_iota(jnp.int32, (tm, tm), 1)).astype(BF16)
    prefix = jnp.dot(onehot.astype(BF16), before, preferred_element_type=F32)
    carry = carry_ref[...]
    rank_all = prefix + jnp.concatenate([carry] * (tm // LANES), axis=1)
    for j in range(TOP_K):
        rank_ref[j:j + 1, :] = jnp.sum(jnp.where(sels[j], rank_all, 0.0), axis=0,
                                       keepdims=True).astype(jnp.int32)
    carry = carry + jnp.sum(onehot, axis=1, keepdims=True)
    carry_ref[...] = carry
    cnt_ref[...] = carry.astype(jnp.int32)


def _router(h, norm_g, router_w, router_b):
    n, d = h.shape
    n_exp = router_w.shape[1]
    tm = min(ROUTER_TM, n)
    out_spec = pl.BlockSpec((TOP_K, tm), lambda i: (0, i))
    return pl.pallas_call(
        _router_kernel,
        grid=(n // tm,),
        in_specs=[
            pl.BlockSpec((tm, d), lambda i: (i, 0)),
            pl.BlockSpec((1, d), lambda i: (0, 0)),
            pl.BlockSpec((n_exp, d), lambda i: (0, 0)),
            pl.BlockSpec((n_exp, 1), lambda i: (0, 0)),
        ],
        out_specs=[out_spec, out_spec, out_spec, pl.BlockSpec((n_exp, LANES), lambda i: (0, 0))],
        out_shape=[
            jax.ShapeDtypeStruct((TOP_K, n), jnp.int32),
            jax.ShapeDtypeStruct((TOP_K, n), F32),
            jax.ShapeDtypeStruct((TOP_K, n), jnp.int32),
            jax.ShapeDtypeStruct((n_exp, LANES), jnp.int32),
        ],
        scratch_shapes=[pltpu.VMEM((n_exp, LANES), F32)],
        compiler_params=_cparams(("arbitrary",)),
        name="router",
    )(h, norm_g.reshape(1, d), jnp.transpose(router_w), router_b.reshape(n_exp, 1))


def _dispatch_kernel(vend_ref, pend_ref, h_ref, g_ref, dest_ref, xs_hbm, hn_ref, sems, zsem, *, nsteps):
    tm = h_ref.shape[0]
    n_exp = vend_ref.shape[0]
    n_rows = xs_hbm.shape[0]
    i = pl.program_id(0)
    slot = i % 2

    def wait_slot(s):
        for _ in range(TOP_K):
            pltpu.make_async_copy(hn_ref.at[s], xs_hbm.at[pl.ds(0, tm)], sems.at[s]).wait()

    @pl.when(i >= 2)
    def _():
        wait_slot(slot)

    hn_ref[slot] = _rms(h_ref[...], g_ref[...])

    def issue(t, carry):
        for j in range(TOP_K):
            d = dest_ref[t * TOP_K + j]
            pltpu.make_async_copy(hn_ref.at[slot, pl.ds(t, 1)], xs_hbm.at[pl.ds(d, 1)],
                                  sems.at[slot]).start()
        return carry

    lax.fori_loop(0, tm, issue, 0, unroll=ISSUE_UNROLL)

    @pl.when(i == nsteps - 1)
    def _():
        wait_slot(slot)
        if nsteps > 1:
            wait_slot(1 - slot)
        hn_ref[0] = jnp.zeros(hn_ref.shape[1:], F32)

        def zero_row(r):
            return pltpu.make_async_copy(hn_ref.at[0, pl.ds(0, 1)], xs_hbm.at[pl.ds(r, 1)], zsem)

        def zero_chunk(c):
            return pltpu.make_async_copy(hn_ref.at[0], xs_hbm.at[pl.ds(c * tm, tm)], zsem)

        def for_range(lo, hi, fn):
            def body(r, carry):
                fn(r)
                return carry

            lax.fori_loop(lo, hi, body, 0)

        def per_expert(e):
            for_range(vend_ref[e], pend_ref[e], lambda r: zero_row(r).start())
            for_range(vend_ref[e], pend_ref[e], lambda r: zero_row(r).wait())

        for_range(0, n_exp, per_expert)
        tail_lo = pend_ref[n_exp - 1] // tm
        for_range(tail_lo, n_rows // tm, lambda c: zero_chunk(c).start())
        for_range(tail_lo, n_rows // tm, lambda c: zero_chunk(c).wait())


def _dispatch(h, norm_g, dest_flat, valid_end, pad_end, n_rows):
    n, d = h.shape
    tm = min(ROW_TM, n)
    assert EXP_TM % tm == 0
    nsteps = n // tm
    grid_spec = pltpu.PrefetchScalarGridSpec(
        num_scalar_prefetch=2,
        grid=(nsteps,),
        in_specs=[
            pl.BlockSpec((tm, d), lambda i, *_: (i, 0)),
            pl.BlockSpec((1, d), lambda i, *_: (0, 0)),
            pl.BlockSpec((tm * TOP_K,), lambda i, *_: (i,), memory_space=pltpu.SMEM),
        ],
        out_specs=pl.BlockSpec(memory_space=pl.ANY),
        scratch_shapes=[
            pltpu.VMEM((2, tm, d), F32),
            pltpu.SemaphoreType.DMA((2,)),
            pltpu.SemaphoreType.DMA(()),
        ],
    )
    return pl.pallas_call(
        functools.partial(_dispatch_kernel, nsteps=nsteps),
        grid_spec=grid_spec,
        out_shape=jax.ShapeDtypeStruct((n_rows, d), F32),
        compiler_params=_cparams(("arbitrary",)),
        name="dispatch",
    )(valid_end, pad_end, h, norm_g.reshape(1, d), dest_flat)


def _gate_up_kernel(be_ref, xb_ref, nu_ref, x_ref, wg_ref, wu_ref, bg_ref, bu_ref, o_ref):
    i = pl.program_id(1)

    @pl.when(i < nu_ref[0])
    def _():
        x = x_ref[...]
        g = jnp.dot(x, wg_ref[...], preferred_element_type=F32) + bg_ref[...]
        u = jnp.dot(x, wu_ref[...], preferred_element_type=F32) + bu_ref[...]
        g = jnp.minimum(g, SWIGLU_LIMIT)
        u = jnp.clip(u, -SWIGLU_LIMIT, SWIGLU_LIMIT)
        o_ref[...] = g * jax.nn.sigmoid(SWIGLU_ALPHA * g) * (u + 1)

    @pl.when(i >= nu_ref[0])
    def _():
        o_ref[...] = jnp.zeros_like(o_ref)


def _gate_up(xs, w_gu, b_gu, layer, block_e, x_block, n_used):
    n_rows, d = xs.shape
    depth, n_exp, _, f2 = w_gu.shape
    f = f2 // 2
    tm, tf = EXP_TM, GU_TF
    nj = f // tf
    b4 = b_gu.reshape(depth, n_exp, 1, f2)
    grid_spec = pltpu.PrefetchScalarGridSpec(
        num_scalar_prefetch=3,
        grid=(nj, n_rows // tm),
        in_specs=[
            pl.BlockSpec((tm, d), lambda j, i, be, xb, nu: (xb[i], 0)),
            pl.BlockSpec((None, None, d, tf), lambda j, i, be, xb, nu: (layer, be[i], 0, j)),
            pl.BlockSpec((None, None, d, tf), lambda j, i, be, xb, nu: (layer, be[i], 0, j + nj)),
            pl.BlockSpec((None, None, 1, tf), lambda j, i, be, xb, nu: (layer, be[i], 0, j)),
            pl.BlockSpec((None, None, 1, tf), lambda j, i, be, xb, nu: (layer, be[i], 0, j + nj)),
        ],
        out_specs=pl.BlockSpec((tm, tf), lambda j, i, be, xb, nu: (i, j)),
    )
    return pl.pallas_call(
        _gate_up_kernel,
        grid_spec=grid_spec,
        out_shape=jax.ShapeDtypeStruct((n_rows, f), F32),
        compiler_params=_cparams(("arbitrary", "arbitrary")),
        name="expert_gate_up",
    )(block_e, x_block, n_used, xs, w_gu, w_gu, b4, b4)


def _down_kernel(be_ref, xb_ref, nu_ref, a_ref, w_ref, b_ref, o_ref):
    i = pl.program_id(0)

    @pl.when(i < nu_ref[0])
    def _():
        o_ref[...] = jnp.dot(a_ref[...], w_ref[...], preferred_element_type=F32) + b_ref[...]

    @pl.when(i >= nu_ref[0])
    def _():
        o_ref[...] = jnp.zeros_like(o_ref)


def _down(act, w_down, b_down, layer, block_e, x_block, n_used):
    n_rows, f = act.shape
    depth, n_exp, _, d = w_down.shape
    tm = EXP_TM
    grid_spec = pltpu.PrefetchScalarGridSpec(
        num_scalar_prefetch=3,
        grid=(n_rows // tm,),
        in_specs=[
            pl.BlockSpec((tm, f), lambda i, be, xb, nu: (xb[i], 0)),
            pl.BlockSpec((None, None, f, d), lambda i, be, xb, nu: (layer, be[i], 0, 0)),
            pl.BlockSpec((None, None, 1, d), lambda i, be, xb, nu: (layer, be[i], 0, 0)),
        ],
        out_specs=pl.BlockSpec((tm, d), lambda i, be, xb, nu: (i, 0)),
    )
    return pl.pallas_call(
        _down_kernel,
        grid_spec=grid_spec,
        out_shape=jax.ShapeDtypeStruct((n_rows, d), F32),
        compiler_params=_cparams(("arbitrary",)),
        name="expert_down",
    )(block_e, x_block, n_used, act, w_down, b_down.reshape(depth, n_exp, 1, d))


def _combine_ple_kernel(dcur_ref, dnext_ref, h_ref, gate_ref, p_ref, wp_ref, png_ref, lng_ref,
                        wg_ref, bg_ref, fing_ref, y_hbm, o_ref, ybuf0_ref, ybuf1_ref, h2_ref, sems,
                        *, final, nsteps):
    tm = h_ref.shape[0] // 2
    rows = tm // COMBINE_GROUPS
    g = pl.program_id(0)

    def start(dref, base, t, buf, sem):
        for j in range(TOP_K):
            r = dref[base + t * TOP_K + j]
            pltpu.make_async_copy(y_hbm.at[pl.ds(r, 1)], buf.at[j, pl.ds(t, 1)], sem).start()

    def wait(buf, sem):
        for j in range(TOP_K):
            pltpu.make_async_copy(y_hbm.at[pl.ds(0, tm)], buf.at[j], sem).wait()

    def tile(buf, lo, issue_group):
        ts = slice(lo, lo + tm)
        for c in range(COMBINE_GROUPS):
            issue_group(c)
            rs = slice(c * rows, (c + 1) * rows)
            gs = slice(lo + c * rows, lo + (c + 1) * rows)
            gates = gate_ref[gs, :]
            moe = gates[:, 0:1] * buf[0, rs, :]
            for j in range(1, TOP_K):
                moe = moe + gates[:, j:j + 1] * buf[j, rs, :]
            h2_ref[rs, :] = h_ref[gs, :] + moe
        h2 = h2_ref[...]
        pe = _rms(jnp.dot(p_ref[ts, :].astype(BF16), wp_ref[...], preferred_element_type=F32),
                  png_ref[...])
        hn = _rms(h2, lng_ref[...]).astype(BF16)
        gate = jax.nn.sigmoid(jnp.dot(hn, wg_ref[...], preferred_element_type=F32) + bg_ref[...])
        out = h2 + gate * pe
        if final:
            out = _rms(out, fing_ref[...])
        o_ref[ts, :] = out

    @pl.when(g == 0)
    def _():
        def body(t, carry):
            start(dcur_ref, 0, t, ybuf0_ref, sems.at[0])
            return carry

        lax.fori_loop(0, tm, body, 0, unroll=ISSUE_UNROLL)

    def issue_second(c):
        for t in range(c * rows, (c + 1) * rows):
            start(dcur_ref, tm * TOP_K, t, ybuf1_ref, sems.at[1])

    def issue_next_first(c):
        @pl.when(g + 1 < nsteps)
        def _():
            for t in range(c * rows, (c + 1) * rows):
                start(dnext_ref, 0, t, ybuf0_ref, sems.at[0])

    wait(ybuf0_ref, sems.at[0])
    tile(ybuf0_ref, 0, issue_second)
    wait(ybuf1_ref, sems.at[1])
    tile(ybuf1_ref, tm, issue_next_first)


def _combine_ple(h, y_sorted, dest_flat, gates_t, p, w_proj, proj_norm_g, norm_g, w_gate, b_gate,
                 final_g, final):
    n, d = h.shape
    pd = p.shape[1]
    tm = min(ROW_TM, n // 2)
    nsteps = n // (2 * tm)
    row = lambda i: (i, 0)
    const = lambda i: (0, 0)
    return pl.pallas_call(
        functools.partial(_combine_ple_kernel, final=final, nsteps=nsteps),
        grid=(nsteps,),
        in_specs=[
            pl.BlockSpec((2 * tm * TOP_K,), lambda i: (i,), memory_space=pltpu.SMEM),
            pl.BlockSpec((2 * tm * TOP_K,), lambda i: (jnp.minimum(i + 1, nsteps - 1),),
                         memory_space=pltpu.SMEM),
            pl.BlockSpec((2 * tm, d), row),
            pl.BlockSpec((2 * tm, TOP_K), row),
            pl.BlockSpec((2 * tm, pd), row),
            pl.BlockSpec((pd, d), const),
            pl.BlockSpec((1, d), const),
            pl.BlockSpec((1, d), const),
            pl.BlockSpec((d, d), const),
            pl.BlockSpec((1, d), const),
            pl.BlockSpec((1, d), const),
            pl.BlockSpec(memory_space=pl.ANY),
        ],
        out_specs=pl.BlockSpec((2 * tm, d), row),
        out_shape=jax.ShapeDtypeStruct((n, d), F32),
        scratch_shapes=[pltpu.VMEM((TOP_K, tm, d), F32), pltpu.VMEM((TOP_K, tm, d), F32),
                        pltpu.VMEM((tm, d), F32), pltpu.SemaphoreType.DMA((2,))],
        compiler_params=_cparams(("arbitrary",)),
        name="combine_ple",
    )(dest_flat, dest_flat, h, gates_t, p, w_proj.astype(BF16), proj_norm_g.reshape(1, d),
      norm_g.reshape(1, d), w_gate.astype(BF16), b_gate.reshape(1, d), final_g.reshape(1, d), y_sorted)


def _moe_ple(h, layer, ffn_g, router_w, router_b, w_gu, b_gu, w_down, b_down, p, w_proj,
             proj_norm_g, ple_g, w_gate, b_gate, final_g, final):
    n, d = h.shape
    n_exp = router_w.shape[1]
    top_idx, gates, rank, counts = _router(h, ffn_g, router_w, router_b)
    counts = counts[:, 0]
    padded = (counts + EXP_TM - 1) // EXP_TM * EXP_TM
    pad_end = jnp.cumsum(padded).astype(jnp.int32)
    pad_start = pad_end - padded
    valid_end = pad_start + counts
    n_rows = -(-(n * TOP_K + n_exp * (EXP_TM - 1)) // EXP_TM) * EXP_TM
    n_blocks = n_rows // EXP_TM
    experts = jnp.arange(n_exp, dtype=jnp.int32)
    start_of = jnp.sum(jnp.where(top_idx[..., None] == experts, pad_start, 0), axis=-1)
    dest_flat = jnp.transpose(start_of + rank).reshape(n * TOP_K)
    n_used = (pad_end[-1] // EXP_TM).astype(jnp.int32)
    x_block = jnp.minimum(jnp.arange(n_blocks, dtype=jnp.int32), n_used - 1)
    block_e = jnp.minimum(
        jnp.sum((pad_end[None, :] <= (x_block * EXP_TM)[:, None]).astype(jnp.int32), axis=1),
        n_exp - 1)
    n_used1 = n_used.reshape(1)

    xs = _dispatch(h, ffn_g, dest_flat, valid_end, pad_end, n_rows)
    act = _gate_up(xs, w_gu, b_gu, layer, block_e, x_block, n_used1)
    ys = _down(act, w_down, b_down, layer, block_e, x_block, n_used1)
    return _combine_ple(h, ys, dest_flat, jnp.transpose(gates), p, w_proj, proj_norm_g, ple_g,
                        w_gate, b_gate, final_g, final)


@jax.jit
def _forward(x, p, mix_norm, ffn_norm, ple_norm, final_norm,
             a_w_in, a_b_in, a_ln_g, a_ln_b, a_w_s, a_b_s, a_w_out, a_b_out,
             c_w_in, c_b_in, c_dw_w, c_dw_b, c_ln_g, c_ln_b, c_w_out, c_b_out,
             router_w, router_b, e_w_gu, e_b_gu, e_w_down, e_b_down,
             ple_w_proj, ple_proj_norm, ple_w_gate, ple_b_gate):
    bsz, seq, d = x.shape
    depth = p.shape[0]
    n = bsz * seq
    h = x.reshape(n, d)
    for i in range(depth):
        j = i // 2
        if i % 2 == 0:
            z = _in_proj_gelu(h, mix_norm[i], a_w_in[j], a_b_in[j])
            h = _sgu_out(z, a_ln_g[j], a_ln_b[j], a_w_s[j], a_b_s[j], a_w_out[j], a_b_out[j], h)
        else:
            z = _in_proj_glu(h, mix_norm[i], c_w_in[j], c_b_in[j])
            h = _conv_out(z, seq, c_dw_w[j], c_dw_b[j], c_ln_g[j], c_ln_b[j], c_w_out[j],
                          c_b_out[j], h)
        h = _moe_ple(h, i, ffn_norm[i], router_w[i], router_b[i], e_w_gu, e_b_gu, e_w_down,
                     e_b_down, p[i].reshape(n, -1), ple_w_proj[i], ple_proj_norm[i],
                     ple_norm[i], ple_w_gate[i], ple_b_gate[i], final_norm, i == depth - 1)
    return h.reshape(bsz, seq, d)


def kernel(x, p, mix_norm, ffn_norm, ple_norm, final_norm, a_w_in, a_b_in, a_ln_g, a_ln_b, a_w_s, a_b_s, a_w_out, a_b_out, c_w_in, c_b_in, c_dw_w, c_dw_b, c_ln_g, c_ln_b, c_w_out, c_b_out, router_w, router_b, e_w_gu, e_b_gu, e_w_down, e_b_down, ple_w_proj, ple_proj_norm, ple_w_gate, ple_b_gate):
    return _forward(x, p, mix_norm, ffn_norm, ple_norm, final_norm,
                    a_w_in, a_b_in, a_ln_g, a_ln_b, a_w_s, a_b_s, a_w_out, a_b_out,
                    c_w_in, c_b_in, c_dw_w, c_dw_b, c_ln_g, c_ln_b, c_w_out, c_b_out,
                    router_w, router_b, e_w_gu, e_b_gu, e_w_down, e_b_down,
                    ple_w_proj, ple_proj_norm, ple_w_gate, ple_b_gate)
```

```python
import functools

import jax
import jax.numpy as jnp
from jax import lax
from jax.experimental import pallas as pl
from jax.experimental.pallas import tpu as pltpu

CHUNK = 128
A_HEADS = 16
CONV_K = 31
TOP_K = 4
SWIGLU_LIMIT = 7.0
SWIGLU_ALPHA = 1.702
RMS_EPS = 1e-5
LN_EPS = 1e-5

LANES = 128
SUBLANES = 8
VMEM_LIMIT_BYTES = 56 * 1024 * 1024

IN_TM = 1024
IN_TN = 1024
MIX_TM = 256
CONV_HALO = 32
ROUTER_TM = 512
ROW_TM = 256
EXP_TM = 512
GU_TF = 1024
ISSUE_UNROLL = 4
COMBINE_GROUPS = 4

F32 = jnp.float32
BF16 = jnp.bfloat16


def _cparams(sems):
    return pltpu.CompilerParams(dimension_semantics=sems, vmem_limit_bytes=VMEM_LIMIT_BYTES)


def _rms(x, g):
    return x * lax.rsqrt(jnp.mean(x * x, axis=-1, keepdims=True) + RMS_EPS) * g


def _layer_norm(x, g, b):
    xc = x - jnp.mean(x, axis=-1, keepdims=True)
    return xc * lax.rsqrt(jnp.mean(xc * xc, axis=-1, keepdims=True) + LN_EPS) * g + b


def _in_proj_gelu_kernel(h_ref, g_ref, w_ref, b_ref, o_ref, hn_ref):
    @pl.when(pl.program_id(1) == 0)
    def _():
        hn_ref[...] = _rms(h_ref[...], g_ref[...]).astype(BF16)

    z = jnp.dot(hn_ref[...], w_ref[...], preferred_element_type=F32) + b_ref[...]
    o_ref[...] = jax.nn.gelu(z).astype(o_ref.dtype)


def _in_proj_glu_kernel(h_ref, g_ref, wa_ref, wg_ref, ba_ref, bg_ref, o_ref, hn_ref):
    @pl.when(pl.program_id(1) == 0)
    def _():
        hn_ref[...] = _rms(h_ref[...], g_ref[...]).astype(BF16)

    hn = hn_ref[...]
    a = jnp.dot(hn, wa_ref[...], preferred_element_type=F32) + ba_ref[...]
    g = jnp.dot(hn, wg_ref[...], preferred_element_type=F32) + bg_ref[...]
    o_ref[...] = (a * jax.nn.sigmoid(g)).astype(o_ref.dtype)


def _in_proj_gelu(h, norm_g, w, b):
    n, d = h.shape
    cols = w.shape[1]
    tm, tn = min(IN_TM, n), IN_TN
    return pl.pallas_call(
        _in_proj_gelu_kernel,
        grid=(n // tm, cols // tn),
        in_specs=[
            pl.BlockSpec((tm, d), lambda i, j: (i, 0)),
            pl.BlockSpec((1, d), lambda i, j: (0, 0)),
            pl.BlockSpec((d, tn), lambda i, j: (0, j)),
            pl.BlockSpec((1, tn), lambda i, j: (0, j)),
        ],
        out_specs=pl.BlockSpec((tm, tn), lambda i, j: (i, j)),
        out_shape=jax.ShapeDtypeStruct((n, cols), BF16),
        scratch_shapes=[pltpu.VMEM((tm, d), BF16)],
        compiler_params=_cparams(("arbitrary", "arbitrary")),
        name="in_proj_gelu",
    )(h, norm_g.reshape(1, d), w.astype(BF16), b.reshape(1, cols))


def _in_proj_glu(h, norm_g, w, b):
    n, d = h.shape
    width = w.shape[1] // 2
    tm, tn = min(IN_TM, n), IN_TN
    nj = width // tn
    wb = w.astype(BF16)
    b2 = b.reshape(1, 2 * width)
    return pl.pallas_call(
        _in_proj_glu_kernel,
        grid=(n // tm, nj),
        in_specs=[
            pl.BlockSpec((tm, d), lambda i, j: (i, 0)),
            pl.BlockSpec((1, d), lambda i, j: (0, 0)),
            pl.BlockSpec((d, tn), lambda i, j: (0, j)),
            pl.BlockSpec((d, tn), lambda i, j: (0, j + nj)),
            pl.BlockSpec((1, tn), lambda i, j: (0, j)),
            pl.BlockSpec((1, tn), lambda i, j: (0, j + nj)),
        ],
        out_specs=pl.BlockSpec((tm, tn), lambda i, j: (i, j)),
        out_shape=jax.ShapeDtypeStruct((n, width), BF16),
        scratch_shapes=[pltpu.VMEM((tm, d), BF16)],
        compiler_params=_cparams(("arbitrary", "arbitrary")),
        name="in_proj_glu",
    )(h, norm_g.reshape(1, d), wb, wb, b2, b2)


def _sgu_out_kernel(u_ref, v_ref, lng_ref, lnb_ref, ws_ref, bsb_ref, wo_ref, bo_ref, h_ref,
                    o_ref, mixed_ref):
    tm = u_ref.shape[0]
    nchunk = tm // CHUNK
    vn = _layer_norm(v_ref[...].astype(F32), lng_ref[...], lnb_ref[...]).astype(BF16)
    tgt = lax.broadcasted_iota(jnp.int32, (CHUNK, CHUNK), 0)
    src = lax.broadcasted_iota(jnp.int32, (CHUNK, CHUNK), 1)
    causal = src <= tgt
    for hd in range(A_HEADS):
        hs = slice(hd * LANES, (hd + 1) * LANES)
        w = jnp.where(causal, ws_ref[hd], 0.0).astype(BF16)
        rhs = jnp.concatenate([vn[c * CHUNK:(c + 1) * CHUNK, hs] for c in range(nchunk)], axis=1)
        mix = jnp.dot(w, rhs, preferred_element_type=F32)
        for c in range(nchunk):
            mixed_ref[c * CHUNK:(c + 1) * CHUNK, hs] = mix[:, c * LANES:(c + 1) * LANES] + bsb_ref[:, hs]
    gated = (u_ref[...].astype(F32) * mixed_ref[...]).astype(BF16)
    o_ref[...] = h_ref[...] + (jnp.dot(gated, wo_ref[...], preferred_element_type=F32) + bo_ref[...])


def _sgu_out(z, ln_g, ln_b, w_s, b_s, w_out, b_out, h):
    n, d = h.shape
    width = z.shape[1] // 2
    assert width // A_HEADS == LANES
    tm = MIX_TM
    bsb = jnp.repeat(jnp.transpose(b_s), LANES, axis=1)
    return pl.pallas_call(
        _sgu_out_kernel,
        grid=(n // tm,),
        in_specs=[
            pl.BlockSpec((tm, width), lambda i: (i, 0)),
            pl.BlockSpec((tm, width), lambda i: (i, 1)),
            pl.BlockSpec((1, width), lambda i: (0, 0)),
            pl.BlockSpec((1, width), lambda i: (0, 0)),
            pl.BlockSpec((A_HEADS, CHUNK, CHUNK), lambda i: (0, 0, 0)),
            pl.BlockSpec((CHUNK, width), lambda i: (0, 0)),
            pl.BlockSpec((width, d), lambda i: (0, 0)),
            pl.BlockSpec((1, d), lambda i: (0, 0)),
            pl.BlockSpec((tm, d), lambda i: (i, 0)),
        ],
        out_specs=pl.BlockSpec((tm, d), lambda i: (i, 0)),
        out_shape=jax.ShapeDtypeStruct((n, d), F32),
        scratch_shapes=[pltpu.VMEM((tm, width), F32)],
        compiler_params=_cparams(("arbitrary",)),
        name="sgu_out",
    )(z, z, ln_g.reshape(1, width), ln_b.reshape(1, width), w_s, bsb, w_out.astype(BF16),
      b_out.reshape(1, d), h)


def _conv_out_kernel(z_ref, halo_ref, dww_ref, dwb_ref, lng_ref, lnb_ref, wo_ref, bo_ref, h_ref,
                     o_ref, zbuf_ref, acc_ref, *, tiles_per_seq):
    tm, width = z_ref.shape
    first = (pl.program_id(0) % tiles_per_seq) == 0
    zbuf_ref[0:CONV_HALO, :] = jnp.where(first, 0.0, halo_ref[...].astype(F32))
    zbuf_ref[CONV_HALO:CONV_HALO + tm, :] = z_ref[...].astype(F32)
    zbuf_ref[CONV_HALO + tm:CONV_HALO + tm + SUBLANES, :] = jnp.zeros((SUBLANES, width), F32)
    first_tap = CONV_HALO - (CONV_K - 1)

    def lane_tile(l, carry):
        ls = pl.ds(pl.multiple_of(l * LANES, LANES), LANES)
        out = None
        for r in range(SUBLANES):
            part = None
            for q in range((CONV_HALO + SUBLANES) // SUBLANES):
                k = q * SUBLANES + r - first_tap
                if 0 <= k < CONV_K:
                    term = dww_ref[k:k + 1, ls] * zbuf_ref[q * SUBLANES:q * SUBLANES + tm + SUBLANES, ls]
                    part = term if part is None else part + term
            shifted = part[r:r + tm, :]
            out = shifted if out is None else out + shifted
        acc_ref[:, ls] = out + dwb_ref[:, ls]
        return carry

    lax.fori_loop(0, width // LANES, lane_tile, 0)
    y = jax.nn.silu(_layer_norm(acc_ref[...], lng_ref[...], lnb_ref[...])).astype(BF16)
    o_ref[...] = h_ref[...] + (jnp.dot(y, wo_ref[...], preferred_element_type=F32) + bo_ref[...])


def _conv_out(z, seq, dw_w, dw_b, ln_g, ln_b, w_out, b_out, h):
    n, d = h.shape
    width = z.shape[1]
    tm = MIX_TM
    assert seq % tm == 0 and tm % CONV_HALO == 0 and CONV_HALO >= CONV_K - 1
    ratio = tm // CONV_HALO
    return pl.pallas_call(
        functools.partial(_conv_out_kernel, tiles_per_seq=seq // tm),
        grid=(n // tm,),
        in_specs=[
            pl.BlockSpec((tm, width), lambda i: (i, 0)),
            pl.BlockSpec((CONV_HALO, width), lambda i: (jnp.maximum(i * ratio - 1, 0), 0)),
            pl.BlockSpec((CONV_K, width), lambda i: (0, 0)),
            pl.BlockSpec((1, width), lambda i: (0, 0)),
            pl.BlockSpec((1, width), lambda i: (0, 0)),
            pl.BlockSpec((1, width), lambda i: (0, 0)),
            pl.BlockSpec((width, d), lambda i: (0, 0)),
            pl.BlockSpec((1, d), lambda i: (0, 0)),
            pl.BlockSpec((tm, d), lambda i: (i, 0)),
        ],
        out_specs=pl.BlockSpec((tm, d), lambda i: (i, 0)),
        out_shape=jax.ShapeDtypeStruct((n, d), F32),
        scratch_shapes=[pltpu.VMEM((CONV_HALO + tm + SUBLANES, width), F32),
                        pltpu.VMEM((tm, width), F32)],
        compiler_params=_cparams(("arbitrary",)),
        name="conv_out",
    )(z, z, dw_w, dw_b.reshape(1, width), ln_g.reshape(1, width), ln_b.reshape(1, width),
      w_out.astype(BF16), b_out.reshape(1, d), h)


def _router_kernel(h_ref, g_ref, rwt_ref, rb_ref, idx_ref, gate_ref, rank_ref, cnt_ref, carry_ref):
    tm = h_ref.shape[0]
    n_exp = rwt_ref.shape[0]

    @pl.when(pl.program_id(0) == 0)
    def _():
        carry_ref[...] = jnp.zeros_like(carry_ref)

    hn = _rms(h_ref[...], g_ref[...])
    logits = lax.dot_general(rwt_ref[...], hn, (((1,), (1,)), ((), ())),
                             preferred_element_type=F32) + rb_ref[...]
    eio = lax.broadcasted_iota(jnp.int32, (n_exp, tm), 0)
    cur = logits
    sels, tops = [], []
    for j in range(TOP_K):
        m = jnp.max(cur, axis=0, keepdims=True)
        idx = jnp.min(jnp.where(cur == m, eio, n_exp), axis=0, keepdims=True)
        sel = eio == idx
        idx_ref[j:j + 1, :] = idx
        sels.append(sel)
        tops.append(m)
        cur = jnp.where(sel, -jnp.inf, cur)
    exps = [jnp.exp(t - tops[0]) for t in tops]
    denom = exps[0] + exps[1] + exps[2] + exps[3]
    for j in range(TOP_K):
        gate_ref[j:j + 1, :] = exps[j] / denom
    onehot = jnp.zeros((n_exp, tm), F32)
    for sel in sels:
        onehot = onehot + sel.astype(F32)
    before = (lax.broadcasted_iota(jnp.int32, (tm, tm), 0)
              < lax.broadcasted_iota(jnp.int32, (tm, tm), 1)).astype(BF16)
    prefix = jnp.dot(onehot.astype(BF16), before, preferred_element_type=F32)
    carry = carry_ref[...]
    rank_all = prefix + jnp.concatenate([carry] * (tm // LANES), axis=1)
    for j in range(TOP_K):
        rank_ref[j:j + 1, :] = jnp.sum(jnp.where(sels[j], rank_all, 0.0), axis=0,
                                       keepdims=True).astype(jnp.int32)
    carry = carry + jnp.sum(onehot, axis=1, keepdims=True)
    carry_ref[...] = carry
    cnt_ref[...] = carry.astype(jnp.int32)


def _router(h, norm_g, router_w, router_b):
    n, d = h.shape
    n_exp = router_w.shape[1]
    tm = min(ROUTER_TM, n)
    out_spec = pl.BlockSpec((TOP_K, tm), lambda i: (0, i))
    return pl.pallas_call(
        _router_kernel,
        grid=(n // tm,),
        in_specs=[
            pl.BlockSpec((tm, d), lambda i: (i, 0)),
            pl.BlockSpec((1, d), lambda i: (0, 0)),
            pl.BlockSpec((n_exp, d), lambda i: (0, 0)),
            pl.BlockSpec((n_exp, 1), lambda i: (0, 0)),
        ],
        out_specs=[out_spec, out_spec, out_spec, pl.BlockSpec((n_exp, LANES), lambda i: (0, 0))],
        out_shape=[
            jax.ShapeDtypeStruct((TOP_K, n), jnp.int32),
            jax.ShapeDtypeStruct((TOP_K, n), F32),
            jax.ShapeDtypeStruct((TOP_K, n), jnp.int32),
            jax.ShapeDtypeStruct((n_exp, LANES), jnp.int32),
        ],
        scratch_shapes=[pltpu.VMEM((n_exp, LANES), F32)],
        compiler_params=_cparams(("arbitrary",)),
        name="router",
    )(h, norm_g.reshape(1, d), jnp.transpose(router_w), router_b.reshape(n_exp, 1))


def _dispatch_kernel(vend_ref, pend_ref, h_ref, g_ref, dest_ref, xs_hbm, hn_ref, sems, zsem, *, nsteps):
    tm = h_ref.shape[0]
    n_exp = vend_ref.shape[0]
    n_rows = xs_hbm.shape[0]
    i = pl.program_id(0)
    slot = i % 2

    def wait_slot(s):
        for _ in range(TOP_K):
            pltpu.make_async_copy(hn_ref.at[s], xs_hbm.at[pl.ds(0, tm)], sems.at[s]).wait()

    @pl.when(i >= 2)
    def _():
        wait_slot(slot)

    hn_ref[slot] = _rms(h_ref[...], g_ref[...])

    def issue(t, carry):
        for j in range(TOP_K):
            d = dest_ref[t * TOP_K + j]
            pltpu.make_async_copy(hn_ref.at[slot, pl.ds(t, 1)], xs_hbm.at[pl.ds(d, 1)],
                                  sems.at[slot]).start()
        return carry

    lax.fori_loop(0, tm, issue, 0, unroll=ISSUE_UNROLL)

    @pl.when(i == nsteps - 1)
    def _():
        wait_slot(slot)
        if nsteps > 1:
            wait_slot(1 - slot)
        hn_ref[0] = jnp.zeros(hn_ref.shape[1:], F32)

        def zero_row(r):
            return pltpu.make_async_copy(hn_ref.at[0, pl.ds(0, 1)], xs_hbm.at[pl.ds(r, 1)], zsem)

        def zero_chunk(c):
            return pltpu.make_async_copy(hn_ref.at[0], xs_hbm.at[pl.ds(c * tm, tm)], zsem)

        def for_range(lo, hi, fn):
            def body(r, carry):
                fn(r)
                return carry

            lax.fori_loop(lo, hi, body, 0)

        def per_expert(e):
            for_range(vend_ref[e], pend_ref[e], lambda r: zero_row(r).start())
            for_range(vend_ref[e], pend_ref[e], lambda r: zero_row(r).wait())

        for_range(0, n_exp, per_expert)
        tail_lo = pend_ref[n_exp - 1] // tm
        for_range(tail_lo, n_rows // tm, lambda c: zero_chunk(c).start())
        for_range(tail_lo, n_rows // tm, lambda c: zero_chunk(c).wait())


def _dispatch(h, norm_g, dest_flat, valid_end, pad_end, n_rows):
    n, d = h.shape
    tm = min(ROW_TM, n)
    assert EXP_TM % tm == 0
    nsteps = n // tm
    grid_spec = pltpu.PrefetchScalarGridSpec(
        num_scalar_prefetch=2,
        grid=(nsteps,),
        in_specs=[
            pl.BlockSpec((tm, d), lambda i, *_: (i, 0)),
            pl.BlockSpec((1, d), lambda i, *_: (0, 0)),
            pl.BlockSpec((tm * TOP_K,), lambda i, *_: (i,), memory_space=pltpu.SMEM),
        ],
        out_specs=pl.BlockSpec(memory_space=pl.ANY),
        scratch_shapes=[
            pltpu.VMEM((2, tm, d), F32),
            pltpu.SemaphoreType.DMA((2,)),
            pltpu.SemaphoreType.DMA(()),
        ],
    )
    return pl.pallas_call(
        functools.partial(_dispatch_kernel, nsteps=nsteps),
        grid_spec=grid_spec,
        out_shape=jax.ShapeDtypeStruct((n_rows, d), F32),
        compiler_params=_cparams(("arbitrary",)),
        name="dispatch",
    )(valid_end, pad_end, h, norm_g.reshape(1, d), dest_flat)


def _grouped_kernel(be_ref, first_ref, nexte_ref, hasnext_ref, nu_ref, x_hbm, w_hbm, b_ref, o_hbm,
                    xbuf, wbuf, obuf, xsem, wsem, osem, *, layer, w_col_offsets, epilogue):
    rows = xbuf.shape[1]
    n_w, _, tcols = wbuf.shape[1:]
    ocols = obuf.shape[2]
    n_blocks = o_hbm.shape[0] // rows
    j = pl.program_id(0)
    n_used = nu_ref[0]

    def w_col(k):
        return pl.multiple_of(j * tcols + w_col_offsets[k], LANES)

    def x_copy(b, slot):
        return pltpu.make_async_copy(x_hbm.at[pl.ds(b * rows, rows)], xbuf.at[slot], xsem.at[slot])

    def w_copy(e, slot, k):
        return pltpu.make_async_copy(w_hbm.at[layer, e, :, pl.ds(w_col(k), tcols)],
                                     wbuf.at[slot, k], wsem.at[slot])

    def o_copy(b, slot):
        return pltpu.make_async_copy(
            obuf.at[slot],
            o_hbm.at[pl.ds(b * rows, rows), pl.ds(pl.multiple_of(j * ocols, LANES), ocols)],
            osem.at[slot])

    def for_range(lo, hi, fn):
        def body(c, carry):
            fn(c)
            return carry

        lax.fori_loop(lo, hi, body, 0)

    x_copy(0, 0).start()
    for k in range(n_w):
        w_copy(be_ref[0], 0, k).start()

    def block(b, ws):
        slot = b & 1
        first = first_ref[b] == 1
        e = be_ref[b]
        ws = jnp.where(first & (b > 0), 1 - ws, ws)

        @pl.when(first)
        def _():
            for k in range(n_w):
                w_copy(e, ws, k).wait()

            @pl.when(hasnext_ref[b] == 1)
            def _():
                for k in range(n_w):
                    w_copy(nexte_ref[b], 1 - ws, k).start()

        x_copy(b, slot).wait()

        @pl.when(b + 1 < n_used)
        def _():
            x_copy(b + 1, 1 - slot).start()

        @pl.when(b >= 2)
        def _():
            o_copy(b - 2, slot).wait()

        x = xbuf[slot]
        accs = [jnp.dot(x, wbuf[ws, k], preferred_element_type=F32)
                + b_ref[pl.ds(e, 1), pl.ds(w_col(k), tcols)] for k in range(n_w)]
        obuf[slot] = epilogue(*accs)
        o_copy(b, slot).start()
        return ws

    lax.fori_loop(0, n_used, block, jnp.int32(0))

    @pl.when(n_used >= 2)
    def _():
        o_copy(n_used - 2, n_used & 1).wait()

    o_copy(n_used - 1, (n_used - 1) & 1).wait()
    obuf[0] = jnp.zeros(obuf.shape[1:], F32)
    for_range(n_used, n_blocks, lambda c: o_copy(c, 0).start())
    for_range(n_used, n_blocks, lambda c: o_copy(c, 0).wait())


def _grouped_matmul(x, w, b, layer, sched, *, tcols, ocols, w_col_offsets, epilogue, name):
    n_rows, kdim = x.shape
    n_w = len(w_col_offsets)
    nj = (w.shape[3] // n_w) // tcols
    rows = EXP_TM
    grid_spec = pltpu.PrefetchScalarGridSpec(
        num_scalar_prefetch=5,
        grid=(nj,),
        in_specs=[
            pl.BlockSpec(memory_space=pl.ANY),
            pl.BlockSpec(memory_space=pl.ANY),
            pl.BlockSpec((None,) + b.shape[1:], lambda j, *_: (layer, 0, 0)),
        ],
        out_specs=pl.BlockSpec(memory_space=pl.ANY),
        scratch_shapes=[
            pltpu.VMEM((2, rows, kdim), F32),
            pltpu.VMEM((2, n_w, kdim, tcols), F32),
            pltpu.VMEM((2, rows, ocols), F32),
            pltpu.SemaphoreType.DMA((2,)),
            pltpu.SemaphoreType.DMA((2,)),
            pltpu.SemaphoreType.DMA((2,)),
        ],
    )
    return pl.pallas_call(
        functools.partial(_grouped_kernel, layer=layer, w_col_offsets=w_col_offsets, epilogue=epilogue),
        grid_spec=grid_spec,
        out_shape=jax.ShapeDtypeStruct((n_rows, nj * ocols), F32),
        compiler_params=_cparams(("arbitrary",)),
        name=name,
    )(*sched, x, w, b)


def _swiglu(g, u):
    g = jnp.minimum(g, SWIGLU_LIMIT)
    u = jnp.clip(u, -SWIGLU_LIMIT, SWIGLU_LIMIT)
    return g * jax.nn.sigmoid(SWIGLU_ALPHA * g) * (u + 1)


def _gate_up(xs, w_gu, b_gu, layer, sched):
    f = w_gu.shape[3] // 2
    return _grouped_matmul(xs, w_gu, b_gu, layer, sched, tcols=GU_TF, ocols=GU_TF,
                           w_col_offsets=(0, f), epilogue=_swiglu, name="expert_gate_up")


def _down(act, w_down, b_down, layer, sched):
    d = w_down.shape[3]
    return _grouped_matmul(act, w_down, b_down, layer, sched, tcols=d, ocols=d,
                           w_col_offsets=(0,), epilogue=lambda y: y, name="expert_down")


def _combine_ple_kernel(dcur_ref, dnext_ref, h_ref, gate_ref, p_ref, wp_ref, png_ref, lng_ref,
                        wg_ref, bg_ref, fing_ref, y_hbm, o_ref, ybuf0_ref, ybuf1_ref, h2_ref, sems,
                        *, final, nsteps):
    tm = h_ref.shape[0] // 2
    rows = tm // COMBINE_GROUPS
    g = pl.program_id(0)

    def start(dref, base, t, buf, sem):
        for j in range(TOP_K):
            r = dref[base + t * TOP_K + j]
            pltpu.make_async_copy(y_hbm.at[pl.ds(r, 1)], buf.at[j, pl.ds(t, 1)], sem).start()

    def wait(buf, sem):
        for j in range(TOP_K):
            pltpu.make_async_copy(y_hbm.at[pl.ds(0, tm)], buf.at[j], sem).wait()

    def tile(buf, lo, issue_group):
        ts = slice(lo, lo + tm)
        for c in range(COMBINE_GROUPS):
            issue_group(c)
            rs = slice(c * rows, (c + 1) * rows)
            gs = slice(lo + c * rows, lo + (c + 1) * rows)
            gates = gate_ref[gs, :]
            moe = gates[:, 0:1] * buf[0, rs, :]
            for j in range(1, TOP_K):
                moe = moe + gates[:, j:j + 1] * buf[j, rs, :]
            h2_ref[rs, :] = h_ref[gs, :] + moe
        h2 = h2_ref[...]
        pe = _rms(jnp.dot(p_ref[ts, :].astype(BF16), wp_ref[...], preferred_element_type=F32),
                  png_ref[...])
        hn = _rms(h2, lng_ref[...]).astype(BF16)
        gate = jax.nn.sigmoid(jnp.dot(hn, wg_ref[...], preferred_element_type=F32) + bg_ref[...])
        out = h2 + gate * pe
        if final:
            out = _rms(out, fing_ref[...])
        o_ref[ts, :] = out

    @pl.when(g == 0)
    def _():
        def body(t, carry):
            start(dcur_ref, 0, t, ybuf0_ref, sems.at[0])
            return carry

        lax.fori_loop(0, tm, body, 0, unroll=ISSUE_UNROLL)

    def issue_second(c):
        for t in range(c * rows, (c + 1) * rows):
            start(dcur_ref, tm * TOP_K, t, ybuf1_ref, sems.at[1])

    def issue_next_first(c):
        @pl.when(g + 1 < nsteps)
        def _():
            for t in range(c * rows, (c + 1) * rows):
                start(dnext_ref, 0, t, ybuf0_ref, sems.at[0])

    wait(ybuf0_ref, sems.at[0])
    tile(ybuf0_ref, 0, issue_second)
    wait(ybuf1_ref, sems.at[1])
    tile(ybuf1_ref, tm, issue_next_first)


def _combine_ple(h, y_sorted, dest_flat, gates_t, p, w_proj, proj_norm_g, norm_g, w_gate, b_gate,
                 final_g, final):
    n, d = h.shape
    pd = p.shape[1]
    tm = min(ROW_TM, n // 2)
    nsteps = n // (2 * tm)
    row = lambda i: (i, 0)
    const = lambda i: (0, 0)
    return pl.pallas_call(
        functools.partial(_combine_ple_kernel, final=final, nsteps=nsteps),
        grid=(nsteps,),
        in_specs=[
            pl.BlockSpec((2 * tm * TOP_K,), lambda i: (i,), memory_space=pltpu.SMEM),
            pl.BlockSpec((2 * tm * TOP_K,), lambda i: (jnp.minimum(i + 1, nsteps - 1),),
                         memory_space=pltpu.SMEM),
            pl.BlockSpec((2 * tm, d), row),
            pl.BlockSpec((2 * tm, TOP_K), row),
            pl.BlockSpec((2 * tm, pd), row),
            pl.BlockSpec((pd, d), const),
            pl.BlockSpec((1, d), const),
            pl.BlockSpec((1, d), const),
            pl.BlockSpec((d, d), const),
            pl.BlockSpec((1, d), const),
            pl.BlockSpec((1, d), const),
            pl.BlockSpec(memory_space=pl.ANY),
        ],
        out_specs=pl.BlockSpec((2 * tm, d), row),
        out_shape=jax.ShapeDtypeStruct((n, d), F32),
        scratch_shapes=[pltpu.VMEM((TOP_K, tm, d), F32), pltpu.VMEM((TOP_K, tm, d), F32),
                        pltpu.VMEM((tm, d), F32), pltpu.SemaphoreType.DMA((2,))],
        compiler_params=_cparams(("arbitrary",)),
        name="combine_ple",
    )(dest_flat, dest_flat, h, gates_t, p, w_proj.astype(BF16), proj_norm_g.reshape(1, d),
      norm_g.reshape(1, d), w_gate.astype(BF16), b_gate.reshape(1, d), final_g.reshape(1, d), y_sorted)


def _moe_ple(h, layer, ffn_g, router_w, router_b, w_gu, b_gu, w_down, b_down, p, w_proj,
             proj_norm_g, ple_g, w_gate, b_gate, final_g, final):
    n, d = h.shape
    n_exp = router_w.shape[1]
    top_idx, gates, rank, counts = _router(h, ffn_g, router_w, router_b)
    counts = counts[:, 0]
    padded = (counts + EXP_TM - 1) // EXP_TM * EXP_TM
    pad_end = jnp.cumsum(padded).astype(jnp.int32)
    pad_start = pad_end - padded
    valid_end = pad_start + counts
    n_rows = -(-(n * TOP_K + n_exp * (EXP_TM - 1)) // EXP_TM) * EXP_TM
    n_blocks = n_rows // EXP_TM
    experts = jnp.arange(n_exp, dtype=jnp.int32)
    start_of = jnp.sum(jnp.where(top_idx[..., None] == experts, pad_start, 0), axis=-1)
    dest_flat = jnp.transpose(start_of + rank).reshape(n * TOP_K)
    n_used = (pad_end[-1] // EXP_TM).astype(jnp.int32)
    blocks = jnp.arange(n_blocks, dtype=jnp.int32)
    x_block = jnp.minimum(blocks, n_used - 1)
    block_e = jnp.minimum(
        jnp.sum((pad_end[None, :] <= (x_block * EXP_TM)[:, None]).astype(jnp.int32), axis=1),
        n_exp - 1)
    first = ((blocks == 0) | (block_e != jnp.roll(block_e, 1))).astype(jnp.int32)
    next_block = jnp.sum(jnp.where(block_e[:, None] == experts, pad_end, 0), axis=1) // EXP_TM
    has_next = (next_block < n_used).astype(jnp.int32)
    next_e = jnp.sum(jnp.where(jnp.minimum(next_block, n_blocks - 1)[:, None] == blocks, block_e, 0),
                     axis=1).astype(jnp.int32)
    sched = (block_e, first, next_e, has_next, n_used.reshape(1))

    xs = _dispatch(h, ffn_g, dest_flat, valid_end, pad_end, n_rows)
    act = _gate_up(xs, w_gu, b_gu, layer, sched)
    ys = _down(act, w_down, b_down, layer, sched)
    return _combine_ple(h, ys, dest_flat, jnp.transpose(gates), p, w_proj, proj_norm_g, ple_g,
                        w_gate, b_gate, final_g, final)


@jax.jit
def _forward(x, p, mix_norm, ffn_norm, ple_norm, final_norm,
             a_w_in, a_b_in, a_ln_g, a_ln_b, a_w_s, a_b_s, a_w_out, a_b_out,
             c_w_in, c_b_in, c_dw_w, c_dw_b, c_ln_g, c_ln_b, c_w_out, c_b_out,
             router_w, router_b, e_w_gu, e_b_gu, e_w_down, e_b_down,
             ple_w_proj, ple_proj_norm, ple_w_gate, ple_b_gate):
    bsz, seq, d = x.shape
    depth = p.shape[0]
    n = bsz * seq
    h = x.reshape(n, d)
    for i in range(depth):
        j = i // 2
        if i % 2 == 0:
            z = _in_proj_gelu(h, mix_norm[i], a_w_in[j], a_b_in[j])
            h = _sgu_out(z, a_ln_g[j], a_ln_b[j], a_w_s[j], a_b_s[j], a_w_out[j], a_b_out[j], h)
        else:
            z = _in_proj_glu(h, mix_norm[i], c_w_in[j], c_b_in[j])
            h = _conv_out(z, seq, c_dw_w[j], c_dw_b[j], c_ln_g[j], c_ln_b[j], c_w_out[j],
                          c_b_out[j], h)
        h = _moe_ple(h, i, ffn_norm[i], router_w[i], router_b[i], e_w_gu, e_b_gu, e_w_down,
                     e_b_down, p[i].reshape(n, -1), ple_w_proj[i], ple_proj_norm[i],
                     ple_norm[i], ple_w_gate[i], ple_b_gate[i], final_norm, i == depth - 1)
    return h.reshape(bsz, seq, d)


def kernel(x, p, mix_norm, ffn_norm, ple_norm, final_norm, a_w_in, a_b_in, a_ln_g, a_ln_b, a_w_s, a_b_s, a_w_out, a_b_out, c_w_in, c_b_in, c_dw_w, c_dw_b, c_ln_g, c_ln_b, c_w_out, c_b_out, router_w, router_b, e_w_gu, e_b_gu, e_w_down, e_b_down, ple_w_proj, ple_proj_norm, ple_w_gate, ple_b_gate):
    return _forward(x, p, mix_norm, ffn_norm, ple_norm, final_norm,
                    a_w_in, a_b_in, a_ln_g, a_ln_b, a_w_s, a_b_s, a_w_out, a_b_out,
                    c_w_in, c_b_in, c_dw_w, c_dw_b, c_ln_g, c_ln_b, c_w_out, c_b_out,
                    router_w, router_b, e_w_gu, e_b_gu, e_w_down, e_b_down,
                    ple_w_proj, ple_proj_norm, ple_w_gate, ple_b_gate)
```

```python
import functools

import jax
import jax.numpy as jnp
from jax import lax
from jax.experimental import pallas as pl
from jax.experimental.pallas import tpu as pltpu

CHUNK = 128
A_HEADS = 16
CONV_K = 31
TOP_K = 4
SWIGLU_LIMIT = 7.0
SWIGLU_ALPHA = 1.702
RMS_EPS = 1e-5
LN_EPS = 1e-5

LANES = 128
SUBLANES = 8
VMEM_LIMIT_BYTES = 56 * 1024 * 1024

IN_TM = 1024
IN_TN = 1024
MIX_TM = 256
CONV_HALO = 32
ROUTER_TM = 512
ROW_TM = 256
EXP_TM = 512
GU_TF = 1024
ISSUE_UNROLL = 4
COMBINE_GROUPS = 4

F32 = jnp.float32
BF16 = jnp.bfloat16


def _cparams(sems):
    return pltpu.CompilerParams(dimension_semantics=sems, vmem_limit_bytes=VMEM_LIMIT_BYTES)


def _rms(x, g):
    return x * lax.rsqrt(jnp.mean(x * x, axis=-1, keepdims=True) + RMS_EPS) * g


def _layer_norm(x, g, b):
    xc = x - jnp.mean(x, axis=-1, keepdims=True)
    return xc * lax.rsqrt(jnp.mean(xc * xc, axis=-1, keepdims=True) + LN_EPS) * g + b


def _in_proj_gelu_kernel(h_ref, g_ref, w_ref, b_ref, o_ref, hn_ref):
    @pl.when(pl.program_id(1) == 0)
    def _():
        hn_ref[...] = _rms(h_ref[...], g_ref[...]).astype(BF16)

    z = jnp.dot(hn_ref[...], w_ref[...], preferred_element_type=F32) + b_ref[...]
    o_ref[...] = jax.nn.gelu(z).astype(o_ref.dtype)


def _in_proj_glu_kernel(h_ref, g_ref, wa_ref, wg_ref, ba_ref, bg_ref, o_ref, hn_ref):
    @pl.when(pl.program_id(1) == 0)
    def _():
        hn_ref[...] = _rms(h_ref[...], g_ref[...]).astype(BF16)

    hn = hn_ref[...]
    a = jnp.dot(hn, wa_ref[...], preferred_element_type=F32) + ba_ref[...]
    g = jnp.dot(hn, wg_ref[...], preferred_element_type=F32) + bg_ref[...]
    o_ref[...] = (a * jax.nn.sigmoid(g)).astype(o_ref.dtype)


def _in_proj_gelu(h, norm_g, w, b):
    n, d = h.shape
    cols = w.shape[1]
    tm, tn = min(IN_TM, n), IN_TN
    return pl.pallas_call(
        _in_proj_gelu_kernel,
        grid=(n // tm, cols // tn),
        in_specs=[
            pl.BlockSpec((tm, d), lambda i, j: (i, 0)),
            pl.BlockSpec((1, d), lambda i, j: (0, 0)),
            pl.BlockSpec((d, tn), lambda i, j: (0, j)),
            pl.BlockSpec((1, tn), lambda i, j: (0, j)),
        ],
        out_specs=pl.BlockSpec((tm, tn), lambda i, j: (i, j)),
        out_shape=jax.ShapeDtypeStruct((n, cols), BF16),
        scratch_shapes=[pltpu.VMEM((tm, d), BF16)],
        compiler_params=_cparams(("arbitrary", "arbitrary")),
        name="in_proj_gelu",
    )(h, norm_g.reshape(1, d), w.astype(BF16), b.reshape(1, cols))


def _in_proj_glu(h, norm_g, w, b):
    n, d = h.shape
    width = w.shape[1] // 2
    tm, tn = min(IN_TM, n), IN_TN
    nj = width // tn
    wb = w.astype(BF16)
    b2 = b.reshape(1, 2 * width)
    return pl.pallas_call(
        _in_proj_glu_kernel,
        grid=(n // tm, nj),
        in_specs=[
            pl.BlockSpec((tm, d), lambda i, j: (i, 0)),
            pl.BlockSpec((1, d), lambda i, j: (0, 0)),
            pl.BlockSpec((d, tn), lambda i, j: (0, j)),
            pl.BlockSpec((d, tn), lambda i, j: (0, j + nj)),
            pl.BlockSpec((1, tn), lambda i, j: (0, j)),
            pl.BlockSpec((1, tn), lambda i, j: (0, j + nj)),
        ],
        out_specs=pl.BlockSpec((tm, tn), lambda i, j: (i, j)),
        out_shape=jax.ShapeDtypeStruct((n, width), BF16),
        scratch_shapes=[pltpu.VMEM((tm, d), BF16)],
        compiler_params=_cparams(("arbitrary", "arbitrary")),
        name="in_proj_glu",
    )(h, norm_g.reshape(1, d), wb, wb, b2, b2)


def _sgu_out_kernel(u_ref, v_ref, lng_ref, lnb_ref, ws_ref, bsb_ref, wo_ref, bo_ref, h_ref,
                    o_ref, mixed_ref):
    tm = u_ref.shape[0]
    nchunk = tm // CHUNK
    vn = _layer_norm(v_ref[...].astype(F32), lng_ref[...], lnb_ref[...]).astype(BF16)
    tgt = lax.broadcasted_iota(jnp.int32, (CHUNK, CHUNK), 0)
    src = lax.broadcasted_iota(jnp.int32, (CHUNK, CHUNK), 1)
    causal = src <= tgt
    for hd in range(A_HEADS):
        hs = slice(hd * LANES, (hd + 1) * LANES)
        w = jnp.where(causal, ws_ref[hd], 0.0).astype(BF16)
        rhs = jnp.concatenate([vn[c * CHUNK:(c + 1) * CHUNK, hs] for c in range(nchunk)], axis=1)
        mix = jnp.dot(w, rhs, preferred_element_type=F32)
        for c in range(nchunk):
            mixed_ref[c * CHUNK:(c + 1) * CHUNK, hs] = mix[:, c * LANES:(c + 1) * LANES] + bsb_ref[:, hs]
    gated = (u_ref[...].astype(F32) * mixed_ref[...]).astype(BF16)
    o_ref[...] = h_ref[...] + (jnp.dot(gated, wo_ref[...], preferred_element_type=F32) + bo_ref[...])


def _sgu_out(z, ln_g, ln_b, w_s, b_s, w_out, b_out, h):
    n, d = h.shape
    width = z.shape[1] // 2
    assert width // A_HEADS == LANES
    tm = MIX_TM
    bsb = jnp.repeat(jnp.transpose(b_s), LANES, axis=1)
    return pl.pallas_call(
        _sgu_out_kernel,
        grid=(n // tm,),
        in_specs=[
            pl.BlockSpec((tm, width), lambda i: (i, 0)),
            pl.BlockSpec((tm, width), lambda i: (i, 1)),
            pl.BlockSpec((1, width), lambda i: (0, 0)),
            pl.BlockSpec((1, width), lambda i: (0, 0)),
            pl.BlockSpec((A_HEADS, CHUNK, CHUNK), lambda i: (0, 0, 0)),
            pl.BlockSpec((CHUNK, width), lambda i: (0, 0)),
            pl.BlockSpec((width, d), lambda i: (0, 0)),
            pl.BlockSpec((1, d), lambda i: (0, 0)),
            pl.BlockSpec((tm, d), lambda i: (i, 0)),
        ],
        out_specs=pl.BlockSpec((tm, d), lambda i: (i, 0)),
        out_shape=jax.ShapeDtypeStruct((n, d), F32),
        scratch_shapes=[pltpu.VMEM((tm, width), F32)],
        compiler_params=_cparams(("arbitrary",)),
        name="sgu_out",
    )(z, z, ln_g.reshape(1, width), ln_b.reshape(1, width), w_s, bsb, w_out.astype(BF16),
      b_out.reshape(1, d), h)


def _conv_out_kernel(z_ref, halo_ref, dww_ref, dwb_ref, lng_ref, lnb_ref, wo_ref, bo_ref, h_ref,
                     o_ref, zbuf_ref, acc_ref, *, tiles_per_seq):
    tm, width = z_ref.shape
    first = (pl.program_id(0) % tiles_per_seq) == 0
    zbuf_ref[0:CONV_HALO, :] = jnp.where(first, 0.0, halo_ref[...].astype(F32))
    zbuf_ref[CONV_HALO:CONV_HALO + tm, :] = z_ref[...].astype(F32)
    zbuf_ref[CONV_HALO + tm:CONV_HALO + tm + SUBLANES, :] = jnp.zeros((SUBLANES, width), F32)
    first_tap = CONV_HALO - (CONV_K - 1)

    def lane_tile(l, carry):
        ls = pl.ds(pl.multiple_of(l * LANES, LANES), LANES)
        out = None
        for r in range(SUBLANES):
            part = None
            for q in range((CONV_HALO + SUBLANES) // SUBLANES):
                k = q * SUBLANES + r - first_tap
                if 0 <= k < CONV_K:
                    term = dww_ref[k:k + 1, ls] * zbuf_ref[q * SUBLANES:q * SUBLANES + tm + SUBLANES, ls]
                    part = term if part is None else part + term
            shifted = part[r:r + tm, :]
            out = shifted if out is None else out + shifted
        acc_ref[:, ls] = out + dwb_ref[:, ls]
        return carry

    lax.fori_loop(0, width // LANES, lane_tile, 0)
    y = jax.nn.silu(_layer_norm(acc_ref[...], lng_ref[...], lnb_ref[...])).astype(BF16)
    o_ref[...] = h_ref[...] + (jnp.dot(y, wo_ref[...], preferred_element_type=F32) + bo_ref[...])


def _conv_out(z, seq, dw_w, dw_b, ln_g, ln_b, w_out, b_out, h):
    n, d = h.shape
    width = z.shape[1]
    tm = MIX_TM
    assert seq % tm == 0 and tm % CONV_HALO == 0 and CONV_HALO >= CONV_K - 1
    ratio = tm // CONV_HALO
    return pl.pallas_call(
        functools.partial(_conv_out_kernel, tiles_per_seq=seq // tm),
        grid=(n // tm,),
        in_specs=[
            pl.BlockSpec((tm, width), lambda i: (i, 0)),
            pl.BlockSpec((CONV_HALO, width), lambda i: (jnp.maximum(i * ratio - 1, 0), 0)),
            pl.BlockSpec((CONV_K, width), lambda i: (0, 0)),
            pl.BlockSpec((1, width), lambda i: (0, 0)),
            pl.BlockSpec((1, width), lambda i: (0, 0)),
            pl.BlockSpec((1, width), lambda i: (0, 0)),
            pl.BlockSpec((width, d), lambda i: (0, 0)),
            pl.BlockSpec((1, d), lambda i: (0, 0)),
            pl.BlockSpec((tm, d), lambda i: (i, 0)),
        ],
        out_specs=pl.BlockSpec((tm, d), lambda i: (i, 0)),
        out_shape=jax.ShapeDtypeStruct((n, d), F32),
        scratch_shapes=[pltpu.VMEM((CONV_HALO + tm + SUBLANES, width), F32),
                        pltpu.VMEM((tm, width), F32)],
        compiler_params=_cparams(("arbitrary",)),
        name="conv_out",
    )(z, z, dw_w, dw_b.reshape(1, width), ln_g.reshape(1, width), ln_b.reshape(1, width),
      w_out.astype(BF16), b_out.reshape(1, d), h)


def _router_kernel(h_ref, g_ref, rwt_ref, rb_ref, idx_ref, gate_ref, rank_ref, cnt_ref, carry_ref):
    tm = h_ref.shape[0]
    n_exp = rwt_ref.shape[0]

    @pl.when(pl.program_id(0) == 0)
    def _():
        carry_ref[...] = jnp.zeros_like(carry_ref)

    hn = _rms(h_ref[...], g_ref[...])
    logits = lax.dot_general(rwt_ref[...], hn, (((1,), (1,)), ((), ())),
                             preferred_element_type=F32) + rb_ref[...]
    eio = lax.broadcasted_iota(jnp.int32, (n_exp, tm), 0)
    cur = logits
    sels, tops = [], []
    for j in range(TOP_K):
        m = jnp.max(cur, axis=0, keepdims=True)
        idx = jnp.min(jnp.where(cur == m, eio, n_exp), axis=0, keepdims=True)
        sel = eio == idx
        idx_ref[j:j + 1, :] = idx
        sels.append(sel)
        tops.append(m)
        cur = jnp.where(sel, -jnp.inf, cur)
    exps = [jnp.exp(t - tops[0]) for t in tops]
    denom = exps[0] + exps[1] + exps[2] + exps[3]
    for j in range(TOP_K):
        gate_ref[j:j + 1, :] = exps[j] / denom
    onehot = jnp.zeros((n_exp, tm), F32)
    for sel in sels:
        onehot = onehot + sel.astype(F32)
    before = (lax.broadcasted_iota(jnp.int32, (tm, tm), 0)
              < lax.broadcasted_iota(jnp.int32, (tm, tm), 1)).astype(BF16)
    prefix = jnp.dot(onehot.astype(BF16), before, preferred_element_type=F32)
    carry = carry_ref[...]
    rank_all = prefix + jnp.concatenate([carry] * (tm // LANES), axis=1)
    for j in range(TOP_K):
        rank_ref[j:j + 1, :] = jnp.sum(jnp.where(sels[j], rank_all, 0.0), axis=0,
                                       keepdims=True).astype(jnp.int32)
    carry = carry + jnp.sum(onehot, axis=1, keepdims=True)
    carry_ref[...] = carry
    cnt_ref[...] = carry.astype(jnp.int32)


def _router(h, norm_g, router_w, router_b):
    n, d = h.shape
    n_exp = router_w.shape[1]
    tm = min(ROUTER_TM, n)
    out_spec = pl.BlockSpec((TOP_K, tm), lambda i: (0, i))
    return pl.pallas_call(
        _router_kernel,
        grid=(n // tm,),
        in_specs=[
            pl.BlockSpec((tm, d), lambda i: (i, 0)),
            pl.BlockSpec((1, d), lambda i: (0, 0)),
            pl.BlockSpec((n_exp, d), lambda i: (0, 0)),
            pl.BlockSpec((n_exp, 1), lambda i: (0, 0)),
        ],
        out_specs=[out_spec, out_spec, out_spec, pl.BlockSpec((n_exp, LANES), lambda i: (0, 0))],
        out_shape=[
            jax.ShapeDtypeStruct((TOP_K, n), jnp.int32),
            jax.ShapeDtypeStruct((TOP_K, n), F32),
            jax.ShapeDtypeStruct((TOP_K, n), jnp.int32),
            jax.ShapeDtypeStruct((n_exp, LANES), jnp.int32),
        ],
        scratch_shapes=[pltpu.VMEM((n_exp, LANES), F32)],
        compiler_params=_cparams(("arbitrary",)),
        name="router",
    )(h, norm_g.reshape(1, d), jnp.transpose(router_w), router_b.reshape(n_exp, 1))


def _pack_bf16_halves(x):
    half = x.shape[1] // 2
    bits = lax.bitcast_convert_type(x.astype(BF16).astype(F32), jnp.uint32)
    return bits[:, half:] | (bits[:, :half] >> 16)


def _unpack_bf16_halves(w):
    lo = lax.bitcast_convert_type(w << 16, F32)
    hi = lax.bitcast_convert_type(w & jnp.uint32(0xFFFF0000), F32)
    return lo, hi


def _dispatch_kernel(vend_ref, pend_ref, h_ref, g_ref, dest_ref, xs_hbm, hn_ref, sems, zsem, *, nsteps):
    tm = h_ref.shape[0]
    n_exp = vend_ref.shape[0]
    n_rows = xs_hbm.shape[0]
    i = pl.program_id(0)
    slot = i % 2

    def wait_slot(s):
        for _ in range(TOP_K):
            pltpu.make_async_copy(hn_ref.at[s], xs_hbm.at[pl.ds(0, tm)], sems.at[s]).wait()

    @pl.when(i >= 2)
    def _():
        wait_slot(slot)

    hn_ref[slot] = _pack_bf16_halves(_rms(h_ref[...], g_ref[...]))

    def issue(t, carry):
        for j in range(TOP_K):
            d = dest_ref[t * TOP_K + j]
            pltpu.make_async_copy(hn_ref.at[slot, pl.ds(t, 1)], xs_hbm.at[pl.ds(d, 1)],
                                  sems.at[slot]).start(priority=j % 2)
        return carry

    lax.fori_loop(0, tm, issue, 0, unroll=ISSUE_UNROLL)

    @pl.when(i == nsteps - 1)
    def _():
        wait_slot(slot)
        if nsteps > 1:
            wait_slot(1 - slot)
        hn_ref[0] = jnp.zeros(hn_ref.shape[1:], jnp.uint32)

        def zero_row(r):
            return pltpu.make_async_copy(hn_ref.at[0, pl.ds(0, 1)], xs_hbm.at[pl.ds(r, 1)], zsem)

        def zero_chunk(c):
            return pltpu.make_async_copy(hn_ref.at[0], xs_hbm.at[pl.ds(c * tm, tm)], zsem)

        def for_range(lo, hi, fn):
            def body(r, carry):
                fn(r)
                return carry

            lax.fori_loop(lo, hi, body, 0)

        def per_expert(e):
            for_range(vend_ref[e], pend_ref[e], lambda r: zero_row(r).start())
            for_range(vend_ref[e], pend_ref[e], lambda r: zero_row(r).wait())

        for_range(0, n_exp, per_expert)
        tail_lo = pend_ref[n_exp - 1] // tm
        for_range(tail_lo, n_rows // tm, lambda c: zero_chunk(c).start())
        for_range(tail_lo, n_rows // tm, lambda c: zero_chunk(c).wait())


def _dispatch(h, norm_g, dest_flat, valid_end, pad_end, n_rows):
    n, d = h.shape
    tm = min(ROW_TM, n)
    assert EXP_TM % tm == 0
    nsteps = n // tm
    grid_spec = pltpu.PrefetchScalarGridSpec(
        num_scalar_prefetch=2,
        grid=(nsteps,),
        in_specs=[
            pl.BlockSpec((tm, d), lambda i, *_: (i, 0)),
            pl.BlockSpec((1, d), lambda i, *_: (0, 0)),
            pl.BlockSpec((tm * TOP_K,), lambda i, *_: (i,), memory_space=pltpu.SMEM),
        ],
        out_specs=pl.BlockSpec(memory_space=pl.ANY),
        scratch_shapes=[
            pltpu.VMEM((2, tm, d // 2), jnp.uint32),
            pltpu.SemaphoreType.DMA((2,)),
            pltpu.SemaphoreType.DMA(()),
        ],
    )
    return pl.pallas_call(
        functools.partial(_dispatch_kernel, nsteps=nsteps),
        grid_spec=grid_spec,
        out_shape=jax.ShapeDtypeStruct((n_rows, d // 2), jnp.uint32),
        compiler_params=_cparams(("arbitrary",)),
        name="dispatch",
    )(valid_end, pad_end, h, norm_g.reshape(1, d), dest_flat)


def _grouped_kernel(be_ref, first_ref, nexte_ref, hasnext_ref, nu_ref, x_hbm, w_hbm, b_ref, o_hbm,
                    xbuf, wbuf, obuf, xsem, wsem, osem, *, layer, w_col_offsets, epilogue, packed_x):
    rows = xbuf.shape[1]
    n_w, _, tcols = wbuf.shape[1:]
    ocols = obuf.shape[2]
    n_blocks = o_hbm.shape[0] // rows
    j = pl.program_id(0)
    n_used = nu_ref[0]

    def w_col(k):
        return pl.multiple_of(j * tcols + w_col_offsets[k], LANES)

    def x_copy(b, slot):
        return pltpu.make_async_copy(x_hbm.at[pl.ds(b * rows, rows)], xbuf.at[slot], xsem.at[slot])

    def w_copy(e, slot, k):
        return pltpu.make_async_copy(w_hbm.at[layer, e, :, pl.ds(w_col(k), tcols)],
                                     wbuf.at[slot, k], wsem.at[slot])

    def o_copy(b, slot):
        return pltpu.make_async_copy(
            obuf.at[slot],
            o_hbm.at[pl.ds(b * rows, rows), pl.ds(pl.multiple_of(j * ocols, LANES), ocols)],
            osem.at[slot])

    def for_range(lo, hi, fn):
        def body(c, carry):
            fn(c)
            return carry

        lax.fori_loop(lo, hi, body, 0)

    x_copy(0, 0).start()
    for k in range(n_w):
        w_copy(be_ref[0], 0, k).start()

    def block(b, ws):
        slot = b & 1
        first = first_ref[b] == 1
        e = be_ref[b]
        ws = jnp.where(first & (b > 0), 1 - ws, ws)

        @pl.when(first)
        def _():
            for k in range(n_w):
                w_copy(e, ws, k).wait()

            @pl.when(hasnext_ref[b] == 1)
            def _():
                for k in range(n_w):
                    w_copy(nexte_ref[b], 1 - ws, k).start()

        x_copy(b, slot).wait()

        @pl.when(b + 1 < n_used)
        def _():
            x_copy(b + 1, 1 - slot).start()

        @pl.when(b >= 2)
        def _():
            o_copy(b - 2, slot).wait()

        if packed_x:
            halves = _unpack_bf16_halves(xbuf[slot])
            kh = wbuf.shape[2] // 2
            prods = [sum(jnp.dot(xh, wbuf[ws, k, pl.ds(i * kh, kh), :], preferred_element_type=F32)
                         for i, xh in enumerate(halves)) for k in range(n_w)]
        else:
            x = xbuf[slot]
            prods = [jnp.dot(x, wbuf[ws, k], preferred_element_type=F32) for k in range(n_w)]
        accs = [prods[k] + b_ref[pl.ds(e, 1), pl.ds(w_col(k), tcols)] for k in range(n_w)]
        obuf[slot] = epilogue(*accs)
        o_copy(b, slot).start()
        return ws

    lax.fori_loop(0, n_used, block, jnp.int32(0))

    @pl.when(n_used >= 2)
    def _():
        o_copy(n_used - 2, n_used & 1).wait()

    o_copy(n_used - 1, (n_used - 1) & 1).wait()
    obuf[0] = jnp.zeros(obuf.shape[1:], F32)
    for_range(n_used, n_blocks, lambda c: o_copy(c, 0).start())
    for_range(n_used, n_blocks, lambda c: o_copy(c, 0).wait())


def _grouped_matmul(x, w, b, layer, sched, *, tcols, ocols, w_col_offsets, epilogue, name):
    n_rows = x.shape[0]
    kdim = w.shape[2]
    packed_x = x.dtype == jnp.uint32
    n_w = len(w_col_offsets)
    nj = (w.shape[3] // n_w) // tcols
    rows = EXP_TM
    grid_spec = pltpu.PrefetchScalarGridSpec(
        num_scalar_prefetch=5,
        grid=(nj,),
        in_specs=[
            pl.BlockSpec(memory_space=pl.ANY),
            pl.BlockSpec(memory_space=pl.ANY),
            pl.BlockSpec((None,) + b.shape[1:], lambda j, *_: (layer, 0, 0)),
        ],
        out_specs=pl.BlockSpec(memory_space=pl.ANY),
        scratch_shapes=[
            pltpu.VMEM((2, rows) + x.shape[1:], x.dtype),
            pltpu.VMEM((2, n_w, kdim, tcols), F32),
            pltpu.VMEM((2, rows, ocols), F32),
            pltpu.SemaphoreType.DMA((2,)),
            pltpu.SemaphoreType.DMA((2,)),
            pltpu.SemaphoreType.DMA((2,)),
        ],
    )
    return pl.pallas_call(
        functools.partial(_grouped_kernel, layer=layer, w_col_offsets=w_col_offsets, epilogue=epilogue,
                          packed_x=packed_x),
        grid_spec=grid_spec,
        out_shape=jax.ShapeDtypeStruct((n_rows, nj * ocols), F32),
        compiler_params=_cparams(("arbitrary",)),
        name=name,
    )(*sched, x, w, b)


def _swiglu(g, u):
    g = jnp.minimum(g, SWIGLU_LIMIT)
    u = jnp.clip(u, -SWIGLU_LIMIT, SWIGLU_LIMIT)
    return g * jax.nn.sigmoid(SWIGLU_ALPHA * g) * (u + 1)


def _gate_up(xs, w_gu, b_gu, layer, sched):
    f = w_gu.shape[3] // 2
    return _grouped_matmul(xs, w_gu, b_gu, layer, sched, tcols=GU_TF, ocols=GU_TF,
                           w_col_offsets=(0, f), epilogue=_swiglu, name="expert_gate_up")


def _down(act, w_down, b_down, layer, sched):
    d = w_down.shape[3]
    return _grouped_matmul(act, w_down, b_down, layer, sched, tcols=d, ocols=d,
                           w_col_offsets=(0,), epilogue=lambda y: y, name="expert_down")


def _combine_ple_kernel(dcur_ref, dnext_ref, h_ref, gate_ref, p_ref, wp_ref, png_ref, lng_ref,
                        wg_ref, bg_ref, fing_ref, y_hbm, o_ref, ybuf0_ref, ybuf1_ref, h2_ref, sems,
                        *, final, nsteps):
    tm = h_ref.shape[0] // 2
    rows = tm // COMBINE_GROUPS
    g = pl.program_id(0)

    def start(dref, base, t, buf, sem):
        for j in range(TOP_K):
            r = dref[base + t * TOP_K + j]
            pltpu.make_async_copy(y_hbm.at[pl.ds(r, 1)], buf.at[j, pl.ds(t, 1)], sem).start()

    def wait(buf, sem):
        for j in range(TOP_K):
            pltpu.make_async_copy(y_hbm.at[pl.ds(0, tm)], buf.at[j], sem).wait()

    def tile(buf, lo, issue_group):
        ts = slice(lo, lo + tm)
        for c in range(COMBINE_GROUPS):
            issue_group(c)
            rs = slice(c * rows, (c + 1) * rows)
            gs = slice(lo + c * rows, lo + (c + 1) * rows)
            gates = gate_ref[gs, :]
            moe = gates[:, 0:1] * buf[0, rs, :]
            for j in range(1, TOP_K):
                moe = moe + gates[:, j:j + 1] * buf[j, rs, :]
            h2_ref[rs, :] = h_ref[gs, :] + moe
        h2 = h2_ref[...]
        pe = _rms(jnp.dot(p_ref[ts, :].astype(BF16), wp_ref[...], preferred_element_type=F32),
                  png_ref[...])
        hn = _rms(h2, lng_ref[...]).astype(BF16)
        gate = jax.nn.sigmoid(jnp.dot(hn, wg_ref[...], preferred_element_type=F32) + bg_ref[...])
        out = h2 + gate * pe
        if final:
            out = _rms(out, fing_ref[...])
        o_ref[ts, :] = out

    @pl.when(g == 0)
    def _():
        def body(t, carry):
            start(dcur_ref, 0, t, ybuf0_ref, sems.at[0])
            return carry

        lax.fori_loop(0, tm, body, 0, unroll=ISSUE_UNROLL)

    def issue_second(c):
        for t in range(c * rows, (c + 1) * rows):
            start(dcur_ref, tm * TOP_K, t, ybuf1_ref, sems.at[1])

    def issue_next_first(c):
        @pl.when(g + 1 < nsteps)
        def _():
            for t in range(c * rows, (c + 1) * rows):
                start(dnext_ref, 0, t, ybuf0_ref, sems.at[0])

    wait(ybuf0_ref, sems.at[0])
    tile(ybuf0_ref, 0, issue_second)
    wait(ybuf1_ref, sems.at[1])
    tile(ybuf1_ref, tm, issue_next_first)


def _combine_ple(h, y_sorted, dest_flat, gates_t, p, w_proj, proj_norm_g, norm_g, w_gate, b_gate,
                 final_g, final):
    n, d = h.shape
    pd = p.shape[1]
    tm = min(ROW_TM, n // 2)
    nsteps = n // (2 * tm)
    row = lambda i: (i, 0)
    const = lambda i: (0, 0)
    return pl.pallas_call(
        functools.partial(_combine_ple_kernel, final=final, nsteps=nsteps),
        grid=(nsteps,),
        in_specs=[
            pl.BlockSpec((2 * tm * TOP_K,), lambda i: (i,), memory_space=pltpu.SMEM),
            pl.BlockSpec((2 * tm * TOP_K,), lambda i: (jnp.minimum(i + 1, nsteps - 1),),
                         memory_space=pltpu.SMEM),
            pl.BlockSpec((2 * tm, d), row),
            pl.BlockSpec((2 * tm, TOP_K), row),
            pl.BlockSpec((2 * tm, pd), row),
            pl.BlockSpec((pd, d), const),
            pl.BlockSpec((1, d), const),
            pl.BlockSpec((1, d), const),
            pl.BlockSpec((d, d), const),
            pl.BlockSpec((1, d), const),
            pl.BlockSpec((1, d), const),
            pl.BlockSpec(memory_space=pl.ANY),
        ],
        out_specs=pl.BlockSpec((2 * tm, d), row),
        out_shape=jax.ShapeDtypeStruct((n, d), F32),
        scratch_shapes=[pltpu.VMEM((TOP_K, tm, d), F32), pltpu.VMEM((TOP_K, tm, d), F32),
                        pltpu.VMEM((tm, d), F32), pltpu.SemaphoreType.DMA((2,))],
        compiler_params=_cparams(("arbitrary",)),
        name="combine_ple",
    )(dest_flat, dest_flat, h, gates_t, p, w_proj.astype(BF16), proj_norm_g.reshape(1, d),
      norm_g.reshape(1, d), w_gate.astype(BF16), b_gate.reshape(1, d), final_g.reshape(1, d), y_sorted)


def _moe_ple(h, layer, ffn_g, router_w, router_b, w_gu, b_gu, w_down, b_down, p, w_proj,
             proj_norm_g, ple_g, w_gate, b_gate, final_g, final):
    n, d = h.shape
    n_exp = router_w.shape[1]
    top_idx, gates, rank, counts = _router(h, ffn_g, router_w, router_b)
    counts = counts[:, 0]
    padded = (counts + EXP_TM - 1) // EXP_TM * EXP_TM
    pad_end = jnp.cumsum(padded).astype(jnp.int32)
    pad_start = pad_end - padded
    valid_end = pad_start + counts
    n_rows = -(-(n * TOP_K + n_exp * (EXP_TM - 1)) // EXP_TM) * EXP_TM
    n_blocks = n_rows // EXP_TM
    experts = jnp.arange(n_exp, dtype=jnp.int32)
    start_of = jnp.sum(jnp.where(top_idx[..., None] == experts, pad_start, 0), axis=-1)
    dest_flat = jnp.transpose(start_of + rank).reshape(n * TOP_K)
    n_used = (pad_end[-1] // EXP_TM).astype(jnp.int32)
    blocks = jnp.arange(n_blocks, dtype=jnp.int32)
    x_block = jnp.minimum(blocks, n_used - 1)
    block_e = jnp.minimum(
        jnp.sum((pad_end[None, :] <= (x_block * EXP_TM)[:, None]).astype(jnp.int32), axis=1),
        n_exp - 1)
    first = ((blocks == 0) | (block_e != jnp.roll(block_e, 1))).astype(jnp.int32)
    next_block = jnp.sum(jnp.where(block_e[:, None] == experts, pad_end, 0), axis=1) // EXP_TM
    has_next = (next_block < n_used).astype(jnp.int32)
    next_e = jnp.sum(jnp.where(jnp.minimum(next_block, n_blocks - 1)[:, None] == blocks, block_e, 0),
                     axis=1).astype(jnp.int32)
    sched = (block_e, first, next_e, has_next, n_used.reshape(1))

    xs = _dispatch(h, ffn_g, dest_flat, valid_end, pad_end, n_rows)
    act = _gate_up(xs, w_gu, b_gu, layer, sched)
    ys = _down(act, w_down, b_down, layer, sched)
    return _combine_ple(h, ys, dest_flat, jnp.transpose(gates), p, w_proj, proj_norm_g, ple_g,
                        w_gate, b_gate, final_g, final)


@jax.jit
def _forward(x, p, mix_norm, ffn_norm, ple_norm, final_norm,
             a_w_in, a_b_in, a_ln_g, a_ln_b, a_w_s, a_b_s, a_w_out, a_b_out,
             c_w_in, c_b_in, c_dw_w, c_dw_b, c_ln_g, c_ln_b, c_w_out, c_b_out,
             router_w, router_b, e_w_gu, e_b_gu, e_w_down, e_b_down,
             ple_w_proj, ple_proj_norm, ple_w_gate, ple_b_gate):
    bsz, seq, d = x.shape
    depth = p.shape[0]
    n = bsz * seq
    h = x.reshape(n, d)
    for i in range(depth):
        j = i // 2
        if i % 2 == 0:
            z = _in_proj_gelu(h, mix_norm[i], a_w_in[j], a_b_in[j])
            h = _sgu_out(z, a_ln_g[j], a_ln_b[j], a_w_s[j], a_b_s[j], a_w_out[j], a_b_out[j], h)
        else:
            z = _in_proj_glu(h, mix_norm[i], c_w_in[j], c_b_in[j])
            h = _conv_out(z, seq, c_dw_w[j], c_dw_b[j], c_ln_g[j], c_ln_b[j], c_w_out[j],
                          c_b_out[j], h)
        h = _moe_ple(h, i, ffn_norm[i], router_w[i], router_b[i], e_w_gu, e_b_gu, e_w_down,
                     e_b_down, p[i].reshape(n, -1), ple_w_proj[i], ple_proj_norm[i],
                     ple_norm[i], ple_w_gate[i], ple_b_gate[i], final_norm, i == depth - 1)
    return h.reshape(bsz, seq, d)


def kernel(x, p, mix_norm, ffn_norm, ple_norm, final_norm, a_w_in, a_b_in, a_ln_g, a_ln_b, a_w_s, a_b_s, a_w_out, a_b_out, c_w_in, c_b_in, c_dw_w, c_dw_b, c_ln_g, c_ln_b, c_w_out, c_b_out, router_w, router_b, e_w_gu, e_b_gu, e_w_down, e_b_down, ple_w_proj, ple_proj_norm, ple_w_gate, ple_b_gate):
    return _forward(x, p, mix_norm, ffn_norm, ple_norm, final_norm,
                    a_w_in, a_b_in, a_ln_g, a_ln_b, a_w_s, a_b_s, a_w_out, a_b_out,
                    c_w_in, c_b_in, c_dw_w, c_dw_b, c_ln_g, c_ln_b, c_w_out, c_b_out,
                    router_w, router_b, e_w_gu, e_b_gu, e_w_down, e_b_down,
                    ple_w_proj, ple_proj_norm, ple_w_gate, ple_b_gate)
```

```python
import functools

import jax
import jax.numpy as jnp
from jax import lax
from jax.experimental import pallas as pl
from jax.experimental.pallas import tpu as pltpu

CHUNK = 128
A_HEADS = 16
CONV_K = 31
TOP_K = 4
SWIGLU_LIMIT = 7.0
SWIGLU_ALPHA = 1.702
RMS_EPS = 1e-5
LN_EPS = 1e-5

LANES = 128
SUBLANES = 8
VMEM_LIMIT_BYTES = 56 * 1024 * 1024

IN_TM = 1024
IN_TN = 1024
MIX_TM = 256
CONV_HALO = 32
ROUTER_TM = 512
ROW_TM = 256
EXP_TM = 512
GU_TF = 1024
ISSUE_UNROLL = 4
COMBINE_GROUPS = 4

F32 = jnp.float32
BF16 = jnp.bfloat16


def _cparams(sems):
    return pltpu.CompilerParams(dimension_semantics=sems, vmem_limit_bytes=VMEM_LIMIT_BYTES)


def _rms(x, g):
    return x * lax.rsqrt(jnp.mean(x * x, axis=-1, keepdims=True) + RMS_EPS) * g


def _layer_norm(x, g, b):
    xc = x - jnp.mean(x, axis=-1, keepdims=True)
    return xc * lax.rsqrt(jnp.mean(xc * xc, axis=-1, keepdims=True) + LN_EPS) * g + b


def _in_proj_gelu_kernel(h_ref, g_ref, w_ref, b_ref, o_ref, hn_ref):
    @pl.when(pl.program_id(1) == 0)
    def _():
        hn_ref[...] = _rms(h_ref[...], g_ref[...]).astype(BF16)

    z = jnp.dot(hn_ref[...], w_ref[...], preferred_element_type=F32) + b_ref[...]
    o_ref[...] = jax.nn.gelu(z).astype(o_ref.dtype)


def _in_proj_glu_kernel(h_ref, g_ref, wa_ref, wg_ref, ba_ref, bg_ref, o_ref, hn_ref):
    @pl.when(pl.program_id(1) == 0)
    def _():
        hn_ref[...] = _rms(h_ref[...], g_ref[...]).astype(BF16)

    hn = hn_ref[...]
    a = jnp.dot(hn, wa_ref[...], preferred_element_type=F32) + ba_ref[...]
    g = jnp.dot(hn, wg_ref[...], preferred_element_type=F32) + bg_ref[...]
    o_ref[...] = (a * jax.nn.sigmoid(g)).astype(o_ref.dtype)


def _in_proj_gelu(h, norm_g, w, b):
    n, d = h.shape
    cols = w.shape[1]
    tm, tn = min(IN_TM, n), IN_TN
    return pl.pallas_call(
        _in_proj_gelu_kernel,
        grid=(n // tm, cols // tn),
        in_specs=[
            pl.BlockSpec((tm, d), lambda i, j: (i, 0)),
            pl.BlockSpec((1, d), lambda i, j: (0, 0)),
            pl.BlockSpec((d, tn), lambda i, j: (0, j)),
            pl.BlockSpec((1, tn), lambda i, j: (0, j)),
        ],
        out_specs=pl.BlockSpec((tm, tn), lambda i, j: (i, j)),
        out_shape=jax.ShapeDtypeStruct((n, cols), BF16),
        scratch_shapes=[pltpu.VMEM((tm, d), BF16)],
        compiler_params=_cparams(("arbitrary", "arbitrary")),
        name="in_proj_gelu",
    )(h, norm_g.reshape(1, d), w.astype(BF16), b.reshape(1, cols))


def _in_proj_glu(h, norm_g, w, b):
    n, d = h.shape
    width = w.shape[1] // 2
    tm, tn = min(IN_TM, n), IN_TN
    nj = width // tn
    wb = w.astype(BF16)
    b2 = b.reshape(1, 2 * width)
    return pl.pallas_call(
        _in_proj_glu_kernel,
        grid=(n // tm, nj),
        in_specs=[
            pl.BlockSpec((tm, d), lambda i, j: (i, 0)),
            pl.BlockSpec((1, d), lambda i, j: (0, 0)),
            pl.BlockSpec((d, tn), lambda i, j: (0, j)),
            pl.BlockSpec((d, tn), lambda i, j: (0, j + nj)),
            pl.BlockSpec((1, tn), lambda i, j: (0, j)),
            pl.BlockSpec((1, tn), lambda i, j: (0, j + nj)),
        ],
        out_specs=pl.BlockSpec((tm, tn), lambda i, j: (i, j)),
        out_shape=jax.ShapeDtypeStruct((n, width), BF16),
        scratch_shapes=[pltpu.VMEM((tm, d), BF16)],
        compiler_params=_cparams(("arbitrary", "arbitrary")),
        name="in_proj_glu",
    )(h, norm_g.reshape(1, d), wb, wb, b2, b2)


def _sgu_out_kernel(u_ref, v_ref, lng_ref, lnb_ref, ws_ref, bsb_ref, wo_ref, bo_ref, h_ref,
                    o_ref, mixed_ref):
    tm = u_ref.shape[0]
    nchunk = tm // CHUNK
    vn = _layer_norm(v_ref[...].astype(F32), lng_ref[...], lnb_ref[...]).astype(BF16)
    tgt = lax.broadcasted_iota(jnp.int32, (CHUNK, CHUNK), 0)
    src = lax.broadcasted_iota(jnp.int32, (CHUNK, CHUNK), 1)
    causal = src <= tgt
    for hd in range(A_HEADS):
        hs = slice(hd * LANES, (hd + 1) * LANES)
        w = jnp.where(causal, ws_ref[hd], 0.0).astype(BF16)
        rhs = jnp.concatenate([vn[c * CHUNK:(c + 1) * CHUNK, hs] for c in range(nchunk)], axis=1)
        mix = jnp.dot(w, rhs, preferred_element_type=F32)
        for c in range(nchunk):
            mixed_ref[c * CHUNK:(c + 1) * CHUNK, hs] = mix[:, c * LANES:(c + 1) * LANES] + bsb_ref[:, hs]
    gated = (u_ref[...].astype(F32) * mixed_ref[...]).astype(BF16)
    o_ref[...] = h_ref[...] + (jnp.dot(gated, wo_ref[...], preferred_element_type=F32) + bo_ref[...])


def _sgu_out(z, ln_g, ln_b, w_s, b_s, w_out, b_out, h):
    n, d = h.shape
    width = z.shape[1] // 2
    assert width // A_HEADS == LANES
    tm = MIX_TM
    bsb = jnp.repeat(jnp.transpose(b_s), LANES, axis=1)
    return pl.pallas_call(
        _sgu_out_kernel,
        grid=(n // tm,),
        in_specs=[
            pl.BlockSpec((tm, width), lambda i: (i, 0)),
            pl.BlockSpec((tm, width), lambda i: (i, 1)),
            pl.BlockSpec((1, width), lambda i: (0, 0)),
            pl.BlockSpec((1, width), lambda i: (0, 0)),
            pl.BlockSpec((A_HEADS, CHUNK, CHUNK), lambda i: (0, 0, 0)),
            pl.BlockSpec((CHUNK, width), lambda i: (0, 0)),
            pl.BlockSpec((width, d), lambda i: (0, 0)),
            pl.BlockSpec((1, d), lambda i: (0, 0)),
            pl.BlockSpec((tm, d), lambda i: (i, 0)),
        ],
        out_specs=pl.BlockSpec((tm, d), lambda i: (i, 0)),
        out_shape=jax.ShapeDtypeStruct((n, d), F32),
        scratch_shapes=[pltpu.VMEM((tm, width), F32)],
        compiler_params=_cparams(("arbitrary",)),
        name="sgu_out",
    )(z, z, ln_g.reshape(1, width), ln_b.reshape(1, width), w_s, bsb, w_out.astype(BF16),
      b_out.reshape(1, d), h)


def _conv_out_kernel(z_ref, halo_ref, dww_ref, dwb_ref, lng_ref, lnb_ref, wo_ref, bo_ref, h_ref,
                     o_ref, zbuf_ref, acc_ref, *, tiles_per_seq):
    tm, width = z_ref.shape
    first = (pl.program_id(0) % tiles_per_seq) == 0
    zbuf_ref[0:CONV_HALO, :] = jnp.where(first, 0.0, halo_ref[...].astype(F32))
    zbuf_ref[CONV_HALO:CONV_HALO + tm, :] = z_ref[...].astype(F32)
    zbuf_ref[CONV_HALO + tm:CONV_HALO + tm + SUBLANES, :] = jnp.zeros((SUBLANES, width), F32)
    first_tap = CONV_HALO - (CONV_K - 1)

    def lane_tile(l, carry):
        ls = pl.ds(pl.multiple_of(l * LANES, LANES), LANES)
        out = None
        for r in range(SUBLANES):
            part = None
            for q in range((CONV_HALO + SUBLANES) // SUBLANES):
                k = q * SUBLANES + r - first_tap
                if 0 <= k < CONV_K:
                    term = dww_ref[k:k + 1, ls] * zbuf_ref[q * SUBLANES:q * SUBLANES + tm + SUBLANES, ls]
                    part = term if part is None else part + term
            shifted = part[r:r + tm, :]
            out = shifted if out is None else out + shifted
        acc_ref[:, ls] = out + dwb_ref[:, ls]
        return carry

    lax.fori_loop(0, width // LANES, lane_tile, 0)
    y = jax.nn.silu(_layer_norm(acc_ref[...], lng_ref[...], lnb_ref[...])).astype(BF16)
    o_ref[...] = h_ref[...] + (jnp.dot(y, wo_ref[...], preferred_element_type=F32) + bo_ref[...])


def _conv_out(z, seq, dw_w, dw_b, ln_g, ln_b, w_out, b_out, h):
    n, d = h.shape
    width = z.shape[1]
    tm = MIX_TM
    assert seq % tm == 0 and tm % CONV_HALO == 0 and CONV_HALO >= CONV_K - 1
    ratio = tm // CONV_HALO
    return pl.pallas_call(
        functools.partial(_conv_out_kernel, tiles_per_seq=seq // tm),
        grid=(n // tm,),
        in_specs=[
            pl.BlockSpec((tm, width), lambda i: (i, 0)),
            pl.BlockSpec((CONV_HALO, width), lambda i: (jnp.maximum(i * ratio - 1, 0), 0)),
            pl.BlockSpec((CONV_K, width), lambda i: (0, 0)),
            pl.BlockSpec((1, width), lambda i: (0, 0)),
            pl.BlockSpec((1, width), lambda i: (0, 0)),
            pl.BlockSpec((1, width), lambda i: (0, 0)),
            pl.BlockSpec((width, d), lambda i: (0, 0)),
            pl.BlockSpec((1, d), lambda i: (0, 0)),
            pl.BlockSpec((tm, d), lambda i: (i, 0)),
        ],
        out_specs=pl.BlockSpec((tm, d), lambda i: (i, 0)),
        out_shape=jax.ShapeDtypeStruct((n, d), F32),
        scratch_shapes=[pltpu.VMEM((CONV_HALO + tm + SUBLANES, width), F32),
                        pltpu.VMEM((tm, width), F32)],
        compiler_params=_cparams(("arbitrary",)),
        name="conv_out",
    )(z, z, dw_w, dw_b.reshape(1, width), ln_g.reshape(1, width), ln_b.reshape(1, width),
      w_out.astype(BF16), b_out.reshape(1, d), h)


def _router_kernel(h_ref, g_ref, rwt_ref, rb_ref, idx_ref, gate_ref, rank_ref, cnt_ref, carry_ref):
    tm = h_ref.shape[0]
    n_exp = rwt_ref.shape[0]

    @pl.when(pl.program_id(0) == 0)
    def _():
        carry_ref[...] = jnp.zeros_like(carry_ref)

    hn = _rms(h_ref[...], g_ref[...])
    logits = lax.dot_general(rwt_ref[...], hn, (((1,), (1,)), ((), ())),
                             preferred_element_type=F32) + rb_ref[...]
    eio = lax.broadcasted_iota(jnp.int32, (n_exp, tm), 0)
    cur = logits
    sels, tops = [], []
    for j in range(TOP_K):
        m = jnp.max(cur, axis=0, keepdims=True)
        idx = jnp.min(jnp.where(cur == m, eio, n_exp), axis=0, keepdims=True)
        sel = eio == idx
        idx_ref[j:j + 1, :] = idx
        sels.append(sel)
        tops.append(m)
        cur = jnp.where(sel, -jnp.inf, cur)
    exps = [jnp.exp(t - tops[0]) for t in tops]
    denom = exps[0] + exps[1] + exps[2] + exps[3]
    for j in range(TOP_K):
        gate_ref[j:j + 1, :] = exps[j] / denom
    onehot = jnp.zeros((n_exp, tm), F32)
    for sel in sels:
        onehot = onehot + sel.astype(F32)
    before = (lax.broadcasted_iota(jnp.int32, (tm, tm), 0)
              < lax.broadcasted_iota(jnp.int32, (tm, tm), 1)).astype(BF16)
    prefix = jnp.dot(onehot.astype(BF16), before, preferred_element_type=F32)
    carry = carry_ref[...]
    rank_all = prefix + jnp.concatenate([carry] * (tm // LANES), axis=1)
    for j in range(TOP_K):
        rank_ref[j:j + 1, :] = jnp.sum(jnp.where(sels[j], rank_all, 0.0), axis=0,
                                       keepdims=True).astype(jnp.int32)
    carry = carry + jnp.sum(onehot, axis=1, keepdims=True)
    carry_ref[...] = carry
    cnt_ref[...] = carry.astype(jnp.int32)


def _router(h, norm_g, router_w, router_b):
    n, d = h.shape
    n_exp = router_w.shape[1]
    tm = min(ROUTER_TM, n)
    out_spec = pl.BlockSpec((TOP_K, tm), lambda i: (0, i))
    return pl.pallas_call(
        _router_kernel,
        grid=(n // tm,),
        in_specs=[
            pl.BlockSpec((tm, d), lambda i: (i, 0)),
            pl.BlockSpec((1, d), lambda i: (0, 0)),
            pl.BlockSpec((n_exp, d), lambda i: (0, 0)),
            pl.BlockSpec((n_exp, 1), lambda i: (0, 0)),
        ],
        out_specs=[out_spec, out_spec, out_spec, pl.BlockSpec((n_exp, LANES), lambda i: (0, 0))],
        out_shape=[
            jax.ShapeDtypeStruct((TOP_K, n), jnp.int32),
            jax.ShapeDtypeStruct((TOP_K, n), F32),
            jax.ShapeDtypeStruct((TOP_K, n), jnp.int32),
            jax.ShapeDtypeStruct((n_exp, LANES), jnp.int32),
        ],
        scratch_shapes=[pltpu.VMEM((n_exp, LANES), F32)],
        compiler_params=_cparams(("arbitrary",)),
        name="router",
    )(h, norm_g.reshape(1, d), jnp.transpose(router_w), router_b.reshape(n_exp, 1))


def _dispatch_kernel(vend_ref, pend_ref, h_ref, g_ref, dest_ref, xs_hbm, hn_ref, sems, zsem, *, nsteps):
    tm = h_ref.shape[0]
    n_exp = vend_ref.shape[0]
    n_rows = xs_hbm.shape[0]
    i = pl.program_id(0)
    slot = i % 2

    def wait_slot(s):
        for _ in range(TOP_K):
            pltpu.make_async_copy(hn_ref.at[s], xs_hbm.at[pl.ds(0, tm)], sems.at[s]).wait()

    @pl.when(i >= 2)
    def _():
        wait_slot(slot)

    hn_ref[slot] = _rms(h_ref[...], g_ref[...])

    def issue(t, carry):
        for j in range(TOP_K):
            d = dest_ref[t * TOP_K + j]
            pltpu.make_async_copy(hn_ref.at[slot, pl.ds(t, 1)], xs_hbm.at[pl.ds(d, 1)],
                                  sems.at[slot]).start()
        return carry

    lax.fori_loop(0, tm, issue, 0, unroll=ISSUE_UNROLL)

    @pl.when(i == nsteps - 1)
    def _():
        wait_slot(slot)
        if nsteps > 1:
            wait_slot(1 - slot)
        hn_ref[0] = jnp.zeros(hn_ref.shape[1:], F32)

        def zero_row(r):
            return pltpu.make_async_copy(hn_ref.at[0, pl.ds(0, 1)], xs_hbm.at[pl.ds(r, 1)], zsem)

        def zero_chunk(c):
            return pltpu.make_async_copy(hn_ref.at[0], xs_hbm.at[pl.ds(c * tm, tm)], zsem)

        def for_range(lo, hi, fn):
            def body(r, carry):
                fn(r)
                return carry

            lax.fori_loop(lo, hi, body, 0)

        def per_expert(e):
            for_range(vend_ref[e], pend_ref[e], lambda r: zero_row(r).start())
            for_range(vend_ref[e], pend_ref[e], lambda r: zero_row(r).wait())

        for_range(0, n_exp, per_expert)
        tail_lo = pend_ref[n_exp - 1] // tm
        for_range(tail_lo, n_rows // tm, lambda c: zero_chunk(c).start())
        for_range(tail_lo, n_rows // tm, lambda c: zero_chunk(c).wait())


def _dispatch(h, norm_g, dest_flat, valid_end, pad_end, n_rows):
    n, d = h.shape
    tm = min(ROW_TM, n)
    assert EXP_TM % tm == 0
    nsteps = n // tm
    grid_spec = pltpu.PrefetchScalarGridSpec(
        num_scalar_prefetch=2,
        grid=(nsteps,),
        in_specs=[
            pl.BlockSpec((tm, d), lambda i, *_: (i, 0)),
            pl.BlockSpec((1, d), lambda i, *_: (0, 0)),
            pl.BlockSpec((tm * TOP_K,), lambda i, *_: (i,), memory_space=pltpu.SMEM),
        ],
        out_specs=pl.BlockSpec(memory_space=pl.ANY),
        scratch_shapes=[
            pltpu.VMEM((2, tm, d), F32),
            pltpu.SemaphoreType.DMA((2,)),
            pltpu.SemaphoreType.DMA(()),
        ],
    )
    return pl.pallas_call(
        functools.partial(_dispatch_kernel, nsteps=nsteps),
        grid_spec=grid_spec,
        out_shape=jax.ShapeDtypeStruct((n_rows, d), F32),
        compiler_params=_cparams(("arbitrary",)),
        name="dispatch",
    )(valid_end, pad_end, h, norm_g.reshape(1, d), dest_flat)


def _grouped_kernel(be_ref, first_ref, nexte_ref, hasnext_ref, need_ref, nu_ref, x_hbm, w_hbm, b_ref,
                    o_hbm, xbuf, wbuf, obuf, xsem, wsem, osem, *, layer, w_col_offsets, epilogue):
    rows = xbuf.shape[1]
    n_w, _, tcols = wbuf.shape[1:]
    ocols = obuf.shape[2]
    n_blocks = o_hbm.shape[0] // rows
    j = pl.program_id(0)
    n_used = nu_ref[0]

    def w_col(k):
        return pl.multiple_of(j * tcols + w_col_offsets[k], LANES)

    def x_copy(b, slot):
        return pltpu.make_async_copy(x_hbm.at[pl.ds(b * rows, rows)], xbuf.at[slot], xsem.at[slot])

    def w_copy(e, slot, k):
        return pltpu.make_async_copy(w_hbm.at[layer, e, :, pl.ds(w_col(k), tcols)],
                                     wbuf.at[slot, k], wsem.at[slot])

    def o_copy(b, slot):
        return pltpu.make_async_copy(
            obuf.at[slot],
            o_hbm.at[pl.ds(b * rows, rows), pl.ds(pl.multiple_of(j * ocols, LANES), ocols)],
            osem.at[slot])

    def for_range(lo, hi, fn):
        def body(c, carry):
            fn(c)
            return carry

        lax.fori_loop(lo, hi, body, 0)

    x_copy(0, 0).start()
    for k in range(n_w):
        w_copy(be_ref[0], 0, k).start()
    obuf[...] = jnp.zeros(obuf.shape, F32)

    def block(b, ws):
        slot = b & 1
        first = first_ref[b] == 1
        e = be_ref[b]
        ws = jnp.where(first & (b > 0), 1 - ws, ws)

        @pl.when(first)
        def _():
            for k in range(n_w):
                w_copy(e, ws, k).wait()

            @pl.when(hasnext_ref[b] == 1)
            def _():
                for k in range(n_w):
                    w_copy(nexte_ref[b], 1 - ws, k).start()

        x_copy(b, slot).wait()

        @pl.when(b + 1 < n_used)
        def _():
            x_copy(b + 1, 1 - slot).start()

        @pl.when(b >= 2)
        def _():
            o_copy(b - 2, slot).wait()

        need = need_ref[b]
        for m in range(CHUNK, rows + 1, CHUNK):
            @pl.when(need == m)
            def _(m=m):
                x = xbuf[slot, pl.ds(0, m), :]
                accs = [jnp.dot(x, wbuf[ws, k], preferred_element_type=F32)
                        + b_ref[pl.ds(e, 1), pl.ds(w_col(k), tcols)] for k in range(n_w)]
                obuf[slot, pl.ds(0, m), :] = epilogue(*accs)

        o_copy(b, slot).start()
        return ws

    lax.fori_loop(0, n_used, block, jnp.int32(0))

    @pl.when(n_used >= 2)
    def _():
        o_copy(n_used - 2, n_used & 1).wait()

    o_copy(n_used - 1, (n_used - 1) & 1).wait()
    obuf[0] = jnp.zeros(obuf.shape[1:], F32)
    for_range(n_used, n_blocks, lambda c: o_copy(c, 0).start())
    for_range(n_used, n_blocks, lambda c: o_copy(c, 0).wait())


def _grouped_matmul(x, w, b, layer, sched, *, tcols, ocols, w_col_offsets, epilogue, name):
    n_rows, kdim = x.shape
    n_w = len(w_col_offsets)
    nj = (w.shape[3] // n_w) // tcols
    rows = EXP_TM
    grid_spec = pltpu.PrefetchScalarGridSpec(
        num_scalar_prefetch=6,
        grid=(nj,),
        in_specs=[
            pl.BlockSpec(memory_space=pl.ANY),
            pl.BlockSpec(memory_space=pl.ANY),
            pl.BlockSpec((None,) + b.shape[1:], lambda j, *_: (layer, 0, 0)),
        ],
        out_specs=pl.BlockSpec(memory_space=pl.ANY),
        scratch_shapes=[
            pltpu.VMEM((2, rows, kdim), F32),
            pltpu.VMEM((2, n_w, kdim, tcols), F32),
            pltpu.VMEM((2, rows, ocols), F32),
            pltpu.SemaphoreType.DMA((2,)),
            pltpu.SemaphoreType.DMA((2,)),
            pltpu.SemaphoreType.DMA((2,)),
        ],
    )
    return pl.pallas_call(
        functools.partial(_grouped_kernel, layer=layer, w_col_offsets=w_col_offsets, epilogue=epilogue),
        grid_spec=grid_spec,
        out_shape=jax.ShapeDtypeStruct((n_rows, nj * ocols), F32),
        compiler_params=_cparams(("arbitrary",)),
        name=name,
    )(*sched, x, w, b)


def _swiglu(g, u):
    g = jnp.minimum(g, SWIGLU_LIMIT)
    u = jnp.clip(u, -SWIGLU_LIMIT, SWIGLU_LIMIT)
    return g * jax.nn.sigmoid(SWIGLU_ALPHA * g) * (u + 1)


def _gate_up(xs, w_gu, b_gu, layer, sched):
    f = w_gu.shape[3] // 2
    return _grouped_matmul(xs, w_gu, b_gu, layer, sched, tcols=GU_TF, ocols=GU_TF,
                           w_col_offsets=(0, f), epilogue=_swiglu, name="expert_gate_up")


def _down(act, w_down, b_down, layer, sched):
    d = w_down.shape[3]
    return _grouped_matmul(act, w_down, b_down, layer, sched, tcols=d, ocols=d,
                           w_col_offsets=(0,), epilogue=lambda y: y, name="expert_down")


def _combine_ple_kernel(dcur_ref, dnext_ref, h_ref, gate_ref, p_ref, wp_ref, png_ref, lng_ref,
                        wg_ref, bg_ref, fing_ref, y_hbm, o_ref, ybuf0_ref, ybuf1_ref, h2_ref, sems,
                        *, final, nsteps):
    tm = h_ref.shape[0] // 2
    rows = tm // COMBINE_GROUPS
    g = pl.program_id(0)

    def start(dref, base, t, buf, sem):
        for j in range(TOP_K):
            r = dref[base + t * TOP_K + j]
            pltpu.make_async_copy(y_hbm.at[pl.ds(r, 1)], buf.at[j, pl.ds(t, 1)], sem).start()

    def wait(buf, sem):
        for j in range(TOP_K):
            pltpu.make_async_copy(y_hbm.at[pl.ds(0, tm)], buf.at[j], sem).wait()

    def tile(buf, lo, issue_group):
        ts = slice(lo, lo + tm)
        for c in range(COMBINE_GROUPS):
            issue_group(c)
            rs = slice(c * rows, (c + 1) * rows)
            gs = slice(lo + c * rows, lo + (c + 1) * rows)
            gates = gate_ref[gs, :]
            moe = gates[:, 0:1] * buf[0, rs, :]
            for j in range(1, TOP_K):
                moe = moe + gates[:, j:j + 1] * buf[j, rs, :]
            h2_ref[rs, :] = h_ref[gs, :] + moe
        h2 = h2_ref[...]
        pe = _rms(jnp.dot(p_ref[ts, :].astype(BF16), wp_ref[...], preferred_element_type=F32),
                  png_ref[...])
        hn = _rms(h2, lng_ref[...]).astype(BF16)
        gate = jax.nn.sigmoid(jnp.dot(hn, wg_ref[...], preferred_element_type=F32) + bg_ref[...])
        out = h2 + gate * pe
        if final:
            out = _rms(out, fing_ref[...])
        o_ref[ts, :] = out

    @pl.when(g == 0)
    def _():
        def body(t, carry):
            start(dcur_ref, 0, t, ybuf0_ref, sems.at[0])
            return carry

        lax.fori_loop(0, tm, body, 0, unroll=ISSUE_UNROLL)

    def issue_second(c):
        for t in range(c * rows, (c + 1) * rows):
            start(dcur_ref, tm * TOP_K, t, ybuf1_ref, sems.at[1])

    def issue_next_first(c):
        @pl.when(g + 1 < nsteps)
        def _():
            for t in range(c * rows, (c + 1) * rows):
                start(dnext_ref, 0, t, ybuf0_ref, sems.at[0])

    wait(ybuf0_ref, sems.at[0])
    tile(ybuf0_ref, 0, issue_second)
    wait(ybuf1_ref, sems.at[1])
    tile(ybuf1_ref, tm, issue_next_first)


def _combine_ple(h, y_sorted, dest_flat, gates_t, p, w_proj, proj_norm_g, norm_g, w_gate, b_gate,
                 final_g, final):
    n, d = h.shape
    pd = p.shape[1]
    tm = min(ROW_TM, n // 2)
    nsteps = n // (2 * tm)
    row = lambda i: (i, 0)
    const = lambda i: (0, 0)
    return pl.pallas_call(
        functools.partial(_combine_ple_kernel, final=final, nsteps=nsteps),
        grid=(nsteps,),
        in_specs=[
            pl.BlockSpec((2 * tm * TOP_K,), lambda i: (i,), memory_space=pltpu.SMEM),
            pl.BlockSpec((2 * tm * TOP_K,), lambda i: (jnp.minimum(i + 1, nsteps - 1),),
                         memory_space=pltpu.SMEM),
            pl.BlockSpec((2 * tm, d), row),
            pl.BlockSpec((2 * tm, TOP_K), row),
            pl.BlockSpec((2 * tm, pd), row),
            pl.BlockSpec((pd, d), const),
            pl.BlockSpec((1, d), const),
            pl.BlockSpec((1, d), const),
            pl.BlockSpec((d, d), const),
            pl.BlockSpec((1, d), const),
            pl.BlockSpec((1, d), const),
            pl.BlockSpec(memory_space=pl.ANY),
        ],
        out_specs=pl.BlockSpec((2 * tm, d), row),
        out_shape=jax.ShapeDtypeStruct((n, d), F32),
        scratch_shapes=[pltpu.VMEM((TOP_K, tm, d), F32), pltpu.VMEM((TOP_K, tm, d), F32),
                        pltpu.VMEM((tm, d), F32), pltpu.SemaphoreType.DMA((2,))],
        compiler_params=_cparams(("arbitrary",)),
        name="combine_ple",
    )(dest_flat, dest_flat, h, gates_t, p, w_proj.astype(BF16), proj_norm_g.reshape(1, d),
      norm_g.reshape(1, d), w_gate.astype(BF16), b_gate.reshape(1, d), final_g.reshape(1, d), y_sorted)


def _moe_ple(h, layer, ffn_g, router_w, router_b, w_gu, b_gu, w_down, b_down, p, w_proj,
             proj_norm_g, ple_g, w_gate, b_gate, final_g, final):
    n, d = h.shape
    n_exp = router_w.shape[1]
    top_idx, gates, rank, counts = _router(h, ffn_g, router_w, router_b)
    counts = counts[:, 0]
    padded = (counts + EXP_TM - 1) // EXP_TM * EXP_TM
    pad_end = jnp.cumsum(padded).astype(jnp.int32)
    pad_start = pad_end - padded
    valid_end = pad_start + counts
    n_rows = -(-(n * TOP_K + n_exp * (EXP_TM - 1)) // EXP_TM) * EXP_TM
    n_blocks = n_rows // EXP_TM
    experts = jnp.arange(n_exp, dtype=jnp.int32)
    start_of = jnp.sum(jnp.where(top_idx[..., None] == experts, pad_start, 0), axis=-1)
    dest_flat = jnp.transpose(start_of + rank).reshape(n * TOP_K)
    n_used = (pad_end[-1] // EXP_TM).astype(jnp.int32)
    blocks = jnp.arange(n_blocks, dtype=jnp.int32)
    x_block = jnp.minimum(blocks, n_used - 1)
    block_e = jnp.minimum(
        jnp.sum((pad_end[None, :] <= (x_block * EXP_TM)[:, None]).astype(jnp.int32), axis=1),
        n_exp - 1)
    first = ((blocks == 0) | (block_e != jnp.roll(block_e, 1))).astype(jnp.int32)
    next_block = jnp.sum(jnp.where(block_e[:, None] == experts, pad_end, 0), axis=1) // EXP_TM
    has_next = (next_block < n_used).astype(jnp.int32)
    next_e = jnp.sum(jnp.where(jnp.minimum(next_block, n_blocks - 1)[:, None] == blocks, block_e, 0),
                     axis=1).astype(jnp.int32)
    group_valid_end = jnp.sum(jnp.where(block_e[:, None] == experts, valid_end, 0), axis=1)
    need = jnp.clip(-(-(group_valid_end - x_block * EXP_TM) // CHUNK) * CHUNK, CHUNK, EXP_TM).astype(jnp.int32)
    sched = (block_e, first, next_e, has_next, need, n_used.reshape(1))

    xs = _dispatch(h, ffn_g, dest_flat, valid_end, pad_end, n_rows)
    act = _gate_up(xs, w_gu, b_gu, layer, sched)
    ys = _down(act, w_down, b_down, layer, sched)
    return _combine_ple(h, ys, dest_flat, jnp.transpose(gates), p, w_proj, proj_norm_g, ple_g,
                        w_gate, b_gate, final_g, final)


@jax.jit
def _forward(x, p, mix_norm, ffn_norm, ple_norm, final_norm,
             a_w_in, a_b_in, a_ln_g, a_ln_b, a_w_s, a_b_s, a_w_out, a_b_out,
             c_w_in, c_b_in, c_dw_w, c_dw_b, c_ln_g, c_ln_b, c_w_out, c_b_out,
             router_w, router_b, e_w_gu, e_b_gu, e_w_down, e_b_down,
             ple_w_proj, ple_proj_norm, ple_w_gate, ple_b_gate):
    bsz, seq, d = x.shape
    depth = p.shape[0]
    n = bsz * seq
    h = x.reshape(n, d)
    for i in range(depth):
        j = i // 2
        if i % 2 == 0:
            z = _in_proj_gelu(h, mix_norm[i], a_w_in[j], a_b_in[j])
            h = _sgu_out(z, a_ln_g[j], a_ln_b[j], a_w_s[j], a_b_s[j], a_w_out[j], a_b_out[j], h)
        else:
            z = _in_proj_glu(h, mix_norm[i], c_w_in[j], c_b_in[j])
            h = _conv_out(z, seq, c_dw_w[j], c_dw_b[j], c_ln_g[j], c_ln_b[j], c_w_out[j],
                          c_b_out[j], h)
        h = _moe_ple(h, i, ffn_norm[i], router_w[i], router_b[i], e_w_gu, e_b_gu, e_w_down,
                     e_b_down, p[i].reshape(n, -1), ple_w_proj[i], ple_proj_norm[i],
                     ple_norm[i], ple_w_gate[i], ple_b_gate[i], final_norm, i == depth - 1)
    return h.reshape(bsz, seq, d)


def kernel(x, p, mix_norm, ffn_norm, ple_norm, final_norm, a_w_in, a_b_in, a_ln_g, a_ln_b, a_w_s, a_b_s, a_w_out, a_b_out, c_w_in, c_b_in, c_dw_w, c_dw_b, c_ln_g, c_ln_b, c_w_out, c_b_out, router_w, router_b, e_w_gu, e_b_gu, e_w_down, e_b_down, ple_w_proj, ple_proj_norm, ple_w_gate, ple_b_gate):
    return _forward(x, p, mix_norm, ffn_norm, ple_norm, final_norm,
                    a_w_in, a_b_in, a_ln_g, a_ln_b, a_w_s, a_b_s, a_w_out, a_b_out,
                    c_w_in, c_b_in, c_dw_w, c_dw_b, c_ln_g, c_ln_b, c_w_out, c_b_out,
                    router_w, router_b, e_w_gu, e_b_gu, e_w_down, e_b_down,
                    ple_w_proj, ple_proj_norm, ple_w_gate, ple_b_gate)
```
